```python
import math
import jax, jax.numpy as jnp
from jax import lax
import numpy as np

D_MODEL = 1024
BATCH = 4
SEQ = 4096
DEPTH = 1

GMLP_CHUNK = 128
GMLP_GROUPS = 8
GMLP_WIDTH = D_MODEL
GMLP_GROUP_DIM = GMLP_WIDTH // GMLP_GROUPS
MOBA_HEADS = 8
MOBA_HEAD_DIM = 128
MOBA_WIDTH = MOBA_HEADS * MOBA_HEAD_DIM
MOBA_BLOCK = 256
MOBA_TOPK = 3
MOBA_Q_CHUNK = 32
FFN_HIDDEN = int(math.ceil(8 * D_MODEL / 3 / 256) * 256)
IN_SPLITS = (GMLP_WIDTH, GMLP_WIDTH, MOBA_WIDTH, MOBA_WIDTH, MOBA_WIDTH, D_MODEL, D_MODEL)
IN_WIDTH = sum(IN_SPLITS)
N_MOD = 6
EPS = 1e-6
NEG = -1e30

kernel_name = "hybrid_gmlp_moba_gated_block"


def _rmsnorm(x, g):
    xf = x.astype(jnp.float32)
    y = xf * lax.rsqrt(jnp.mean(xf * xf, axis=-1, keepdims=True) + EPS)
    return (y * g.astype(jnp.float32)).astype(x.dtype)


def _layernorm(x, g, b):
    xf = x.astype(jnp.float32)
    mu = jnp.mean(xf, axis=-1, keepdims=True)
    var = jnp.mean(jnp.square(xf - mu), axis=-1, keepdims=True)
    y = (xf - mu) * lax.rsqrt(var + EPS)
    return (y * g.astype(jnp.float32) + b.astype(jnp.float32)).astype(x.dtype)


def _modulate(h, shift, scale):
    return h * (1 + scale[:, None, :]) + shift[:, None, :]


def _spatial_gating(u, v, ln_g, ln_b, w_s, b_s):
    B_, S_, _ = v.shape
    v = _layernorm(v, ln_g, ln_b)
    nc = S_ // GMLP_CHUNK
    vg = v.reshape(B_, nc, GMLP_CHUNK, GMLP_GROUPS, GMLP_GROUP_DIM)
    causal = jnp.tril(jnp.ones((GMLP_CHUNK, GMLP_CHUNK), dtype=bool))
    w = w_s * causal.astype(w_s.dtype)[None]
    mixed = jnp.einsum('gts,bcsgd->bctgd', w, vg) + b_s.T[None, None, :, :, None]
    return u * mixed.reshape(B_, S_, GMLP_WIDTH)


def _moba_attention(q, k, v):
    B_, S_, H_, hd = q.shape
    bh = B_ * H_
    nb = -(-S_ // MOBA_BLOCK)
    pad = nb * MOBA_BLOCK - S_
    qh = q.transpose(0, 2, 1, 3).reshape(bh, S_, hd)
    kh = jnp.pad(k.transpose(0, 2, 1, 3).reshape(bh, S_, hd), ((0, 0), (0, pad), (0, 0)))
    vh = jnp.pad(v.transpose(0, 2, 1, 3).reshape(bh, S_, hd), ((0, 0), (0, pad), (0, 0)))
    kb = kh.reshape(bh, nb, MOBA_BLOCK, hd)
    vb = vh.reshape(bh, nb, MOBA_BLOCK, hd)
    kbar = jnp.mean(kb.astype(jnp.float32), axis=2)
    topk = min(MOBA_TOPK, nb)
    scale = hd ** -0.5
    blk_ids = jnp.arange(nb)
    gather_blocks = jax.vmap(lambda blocks, idx: blocks[idx])

    def chunk(ci):
        start = ci * MOBA_Q_CHUNK
        qc = lax.dynamic_slice_in_dim(qh, start, MOBA_Q_CHUNK, axis=1)
        t = start + jnp.arange(MOBA_Q_CHUNK)
        qblk = start // MOBA_BLOCK
        gate = jnp.einsum('nqd,nbd->nqb', qc.astype(jnp.float32), kbar)
        gate = jnp.where((blk_ids < qblk)[None, None, :], gate, NEG)
        _, idx = lax.top_k(gate, topk)
        valid = idx < qblk
        k_sel = gather_blocks(kb, idx)
        v_sel = gather_blocks(vb, idx)
        s_sel = jnp.einsum('nqd,nqjkd->nqjk', qc, k_sel).astype(jnp.float32) * scale
        s_sel = jnp.where(valid[..., None], s_sel, NEG).reshape(bh, MOBA_Q_CHUNK, topk * MOBA_BLOCK)
        k_own = lax.dynamic_slice_in_dim(kh, qblk * MOBA_BLOCK, MOBA_BLOCK, axis=1)
        v_own = lax.dynamic_slice_in_dim(vh, qblk * MOBA_BLOCK, MOBA_BLOCK, axis=1)
        s_own = jnp.einsum('nqd,nkd->nqk', qc, k_own).astype(jnp.float32) * scale
        kpos = qblk * MOBA_BLOCK + jnp.arange(MOBA_BLOCK)
        s_own = jnp.where((kpos[None, :] <= t[:, None])[None], s_own, NEG)
        p = jax.nn.softmax(jnp.concatenate([s_sel, s_own], axis=-1), axis=-1)
        p_sel = p[..., :topk * MOBA_BLOCK].reshape(bh, MOBA_Q_CHUNK, topk, MOBA_BLOCK).astype(v.dtype)
        p_own = p[..., topk * MOBA_BLOCK:].astype(v.dtype)
        return (jnp.einsum('nqjk,nqjkd->nqd', p_sel, v_sel)
                + jnp.einsum('nqk,nkd->nqd', p_own, v_own))

    out = lax.map(chunk, jnp.arange(S_ // MOBA_Q_CHUNK))
    out = out.transpose(1, 0, 2, 3).reshape(B_, H_, S_, hd).transpose(0, 2, 1, 3)
    return out.reshape(B_, S_, H_ * hd)


def setup_inputs(seed: int = 0) -> dict:
    key = jax.random.key(seed)
    ks = jax.random.split(key, 20)
    f32 = jnp.float32
    L, D = DEPTH, D_MODEL
    nrm = lambda k, shape, s: (jax.random.normal(k, shape, f32) * s)
    return {
        "x": nrm(ks[0], (BATCH, SEQ, D), 1.0),
        "c": nrm(ks[1], (BATCH, D), 1.0),
        "w_ada": nrm(ks[2], (L, D, N_MOD * D), 0.5 * D ** -0.5),
        "b_ada": nrm(ks[3], (L, N_MOD * D), 0.01),
        "norm_mix_g": 1.0 + nrm(ks[4], (L, D), 0.02),
        "w_in": nrm(ks[5], (L, D, IN_WIDTH), D ** -0.5),
        "ln_v_g": 1.0 + nrm(ks[6], (L, GMLP_WIDTH), 0.02),
        "ln_v_b": nrm(ks[7], (L, GMLP_WIDTH), 0.01),
        "w_spatial": nrm(ks[8], (L, GMLP_GROUPS, GMLP_CHUNK, GMLP_CHUNK), GMLP_CHUNK ** -0.5),
        "b_spatial": 1.0 + nrm(ks[9], (L, GMLP_GROUPS, GMLP_CHUNK), 0.01),
        "w_proj_a": nrm(ks[10], (L, GMLP_WIDTH, D), GMLP_WIDTH ** -0.5),
        "w_proj_b": nrm(ks[11], (L, MOBA_WIDTH, D), MOBA_WIDTH ** -0.5),
        "w_out": nrm(ks[12], (L, D, D), D ** -0.5),
        "norm_ffn_g": 1.0 + nrm(ks[13], (L, D), 0.02),
        "w_ffn_gate": nrm(ks[14], (L, D, FFN_HIDDEN), D ** -0.5),
        "w_ffn_up": nrm(ks[15], (L, D, FFN_HIDDEN), D ** -0.5),
        "w_ffn_down": nrm(ks[16], (L, FFN_HIDDEN, D), FFN_HIDDEN ** -0.5),
        "norm_final_g": 1.0 + nrm(ks[17], (D,), 0.02),
    }


def reference(x, c, w_ada, b_ada, norm_mix_g, w_in, ln_v_g, ln_v_b, w_spatial, b_spatial,
              w_proj_a, w_proj_b, w_out, norm_ffn_g, w_ffn_gate, w_ffn_up, w_ffn_down,
              norm_final_g):
    B_, S_, D = x.shape
    split_pts = list(np.cumsum(IN_SPLITS)[:-1])
    c_act = jax.nn.silu(c)
    for l in range(DEPTH):
        mod = c_act @ w_ada[l] + b_ada[l]
        sh_m, sc_m, g_m, sh_f, sc_f, g_f = jnp.split(mod, N_MOD, axis=-1)

        h = _modulate(_rmsnorm(x, norm_mix_g[l]), sh_m, sc_m)
        proj = h @ w_in[l]
        uv_u, uv_v, q, k, v, gate_a, gate_b = jnp.split(proj, split_pts, axis=-1)
        y_a = _spatial_gating(jax.nn.gelu(uv_u), jax.nn.gelu(uv_v),
                              ln_v_g[l], ln_v_b[l], w_spatial[l], b_spatial[l])
        y_b = _moba_attention(q.reshape(B_, S_, MOBA_HEADS, MOBA_HEAD_DIM),
                              k.reshape(B_, S_, MOBA_HEADS, MOBA_HEAD_DIM),
                              v.reshape(B_, S_, MOBA_HEADS, MOBA_HEAD_DIM))
        merged = (jax.nn.sigmoid(gate_a) * (y_a @ w_proj_a[l])
                  + jax.nn.sigmoid(gate_b) * (y_b @ w_proj_b[l]))
        x = x + g_m[:, None, :] * (merged @ w_out[l])

        h = _modulate(_rmsnorm(x, norm_ffn_g[l]), sh_f, sc_f)
        ff = (jax.nn.silu(h @ w_ffn_gate[l]) * (h @ w_ffn_up[l])) @ w_ffn_down[l]
        x = x + g_f[:, None, :] * ff
    return _rmsnorm(x, norm_final_g)
```

```python
import functools
import math

import jax
import jax.numpy as jnp
from jax import lax
from jax.experimental import pallas as pl
from jax.experimental.pallas import tpu as pltpu

F32 = jnp.float32
BF16 = jnp.bfloat16

GMLP_CHUNK = 128
GMLP_GROUPS = 8
MOBA_HEADS = 8
MOBA_HEAD_DIM = 128
MOBA_BLOCK = 256
MOBA_TOPK = 3
N_MOD = 6
N_SECTIONS = 7
EPS = 1e-6
NEG = -1e30
LANES = 128
VMEM_LIMIT = 56 * 1024 * 1024

SH_M, SC_M, G_M, SH_F, SC_F, G_F = range(N_MOD)


def _dot(a, b):
    return jnp.dot(a, b, preferred_element_type=F32)


def _dot_nt(a, b):
    return lax.dot_general(a, b, (((1,), (1,)), ((), ())), preferred_element_type=F32)


def _split_bf16(a):
    hi = a.astype(BF16)
    lo = (a - hi.astype(F32)).astype(BF16)
    return hi, lo


def _adaln_kernel(c_ref, w_ref, b_ref, o_ref):
    c = c_ref[...]
    c_act = c * jax.nn.sigmoid(c)
    c_hi, c_lo = _split_bf16(c_act)
    w_hi, w_lo = _split_bf16(w_ref[...])
    o_ref[...] = _dot(c_hi, w_hi) + (_dot(c_hi, w_lo) + _dot(c_lo, w_hi)) + b_ref[...]


def _adaln(c8, w_ada, b_ada):
    rows, d = c8.shape
    n = w_ada.shape[1]
    tn = d
    return pl.pallas_call(
        _adaln_kernel,
        grid=(n // tn,),
        in_specs=[
            pl.BlockSpec((rows, d), lambda j: (0, 0)),
            pl.BlockSpec((d, tn), lambda j: (0, j)),
            pl.BlockSpec((1, tn), lambda j: (0, j)),
        ],
        out_specs=pl.BlockSpec((rows, tn), lambda j: (0, j)),
        out_shape=jax.ShapeDtypeStruct((rows, n), F32),
        compiler_params=pltpu.CompilerParams(
            dimension_semantics=("arbitrary",), vmem_limit_bytes=VMEM_LIMIT),
        name="adaln",
    )(c8, w_ada, b_ada)


def _modulated_rmsnorm(x, gain_row, shift_row):
    r = lax.rsqrt(jnp.mean(x * x, axis=-1, keepdims=True) + EPS)
    return x * r * gain_row + shift_row


def _proj_kernel(x_ref, mod_ref, g_ref, w_ref, lng_ref, lnb_ref, o_ref, h_scr, *, q_scale):
    j = pl.program_id(1)

    @pl.when(j == 0)
    def _():
        gain = g_ref[...] * (1.0 + mod_ref[SC_M:SC_M + 1, :])
        h = _modulated_rmsnorm(x_ref[...], gain, mod_ref[SH_M:SH_M + 1, :])
        h_scr[...] = h.astype(BF16)

    acc = _dot(h_scr[...], w_ref[...])

    @pl.when(j == 0)
    def _():
        o_ref[...] = jax.nn.gelu(acc).astype(BF16)

    @pl.when(j == 1)
    def _():
        v = jax.nn.gelu(acc)
        mu = jnp.mean(v, axis=-1, keepdims=True)
        vc = v - mu
        var = jnp.mean(vc * vc, axis=-1, keepdims=True)
        o_ref[...] = (vc * lax.rsqrt(var + EPS) * lng_ref[...] + lnb_ref[...]).astype(BF16)

    @pl.when(j == 2)
    def _():
        o_ref[...] = (acc * q_scale).astype(BF16)

    @pl.when((j == 3) | (j == 4))
    def _():
        o_ref[...] = acc.astype(BF16)

    @pl.when(j >= 5)
    def _():
        o_ref[...] = jax.nn.sigmoid(acc).astype(BF16)


def _proj(x2d, mod3, norm_g, w_in_bf, ln_g, ln_b, *, seq, tm):
    m, d = x2d.shape
    tiles_per_batch = seq // tm
    kern = functools.partial(_proj_kernel, q_scale=MOBA_HEAD_DIM ** -0.5)
    return pl.pallas_call(
        kern,
        grid=(m // tm, N_SECTIONS),
        in_specs=[
            pl.BlockSpec((tm, d), lambda i, j: (i, 0)),
            pl.BlockSpec((None, N_MOD, d), lambda i, j: (i // tiles_per_batch, 0, 0)),
            pl.BlockSpec((1, d), lambda i, j: (0, 0)),
            pl.BlockSpec((d, d), lambda i, j: (0, j)),
            pl.BlockSpec((1, d), lambda i, j: (0, 0)),
            pl.BlockSpec((1, d), lambda i, j: (0, 0)),
        ],
        out_specs=pl.BlockSpec((tm, d), lambda i, j: (i, j)),
        out_shape=jax.ShapeDtypeStruct((m, N_SECTIONS * d), BF16),
        scratch_shapes=[pltpu.VMEM((tm, d), BF16)],
        compiler_params=pltpu.CompilerParams(
            dimension_semantics=("arbitrary", "arbitrary"), vmem_limit_bytes=VMEM_LIMIT),
        name="proj",
    )(x2d, mod3, norm_g, w_in_bf, ln_g, ln_b)


def _attn_kernel(q_ref, k_ref, v_ref, o_ref, kbar_scr, *, n_blocks):
    blk = MOBA_BLOCK
    kbar_scr[...] = jnp.zeros_like(kbar_scr)
    for b in range(n_blocks):
        kb = k_ref[b * blk:(b + 1) * blk, :].astype(F32)
        kbar_scr[b:b + 1, :] = jnp.sum(kb, axis=0, keepdims=True) * (1.0 / blk)

    lane = lax.broadcasted_iota(jnp.int32, (blk, LANES), 1).astype(F32)
    row = lax.broadcasted_iota(jnp.int32, (blk, blk), 0)
    col = lax.broadcasted_iota(jnp.int32, (blk, blk), 1)
    causal = col <= row

    def q_block(qb, carry):
        qs = pl.multiple_of(qb * blk, blk)
        qbf = qb.astype(F32)
        q = q_ref[pl.ds(qs, blk), :]

        kb_hi, kb_lo = _split_bf16(kbar_scr[...])
        gate = _dot_nt(q, kb_hi) + _dot_nt(q, kb_lo)
        g = jnp.where(lane < qbf, gate, NEG)
        sel_bias = jnp.full((blk, LANES), NEG, F32)
        for _ in range(MOBA_TOPK):
            gmax = jnp.max(g, axis=1, keepdims=True)
            idx = jnp.min(jnp.where(g == gmax, lane, float(LANES)), axis=1, keepdims=True)
            pick = lane == idx
            sel_bias = jnp.where(pick & (idx < qbf), 0.0, sel_bias)
            g = jnp.where(pick, -jnp.inf, g)

        s = _dot_nt(q, k_ref[pl.ds(qs, blk), :])
        s = jnp.where(causal, s, NEG)
        m = jnp.max(s, axis=1, keepdims=True)
        p = jnp.exp(s - m)
        l = jnp.sum(p, axis=1, keepdims=True)
        acc = _dot(p.astype(BF16), v_ref[pl.ds(qs, blk), :])

        def k_block(kb, state):
            m, l, acc = state
            ks = pl.multiple_of(kb * blk, blk)
            bias = jnp.sum(jnp.where(lane == kb.astype(F32), sel_bias, 0.0), axis=1, keepdims=True)
            s = _dot_nt(q, k_ref[pl.ds(ks, blk), :]) + bias
            m_new = jnp.maximum(m, jnp.max(s, axis=1, keepdims=True))
            alpha = jnp.exp(m - m_new)
            p = jnp.exp(s - m_new)
            l = alpha * l + jnp.sum(p, axis=1, keepdims=True)
            acc = alpha * acc + _dot(p.astype(BF16), v_ref[pl.ds(ks, blk), :])
            return m_new, l, acc

        m, l, acc = lax.fori_loop(0, qb, k_block, (m, l, acc))
        o_ref[pl.ds(qs, blk), :] = (acc / l).astype(BF16)
        return carry

    lax.fori_loop(0, n_blocks, q_block, 0)


def _attention(proj, *, batch, seq, d):
    hd = MOBA_HEAD_DIM
    n_blocks = seq // MOBA_BLOCK
    per_section = d // hd
    q_off, k_off, v_off = 2 * per_section, 3 * per_section, 4 * per_section
    kern = functools.partial(_attn_kernel, n_blocks=n_blocks)
    return pl.pallas_call(
        kern,
        grid=(batch, MOBA_HEADS),
        in_specs=[
            pl.BlockSpec((seq, hd), lambda b, h: (b, q_off + h)),
            pl.BlockSpec((seq, hd), lambda b, h: (b, k_off + h)),
            pl.BlockSpec((seq, hd), lambda b, h: (b, v_off + h)),
        ],
        out_specs=pl.BlockSpec((seq, hd), lambda b, h: (b, h)),
        out_shape=jax.ShapeDtypeStruct((batch * seq, MOBA_HEADS * hd), BF16),
        scratch_shapes=[pltpu.VMEM((LANES, hd), F32)],
        compiler_params=pltpu.CompilerParams(
            dimension_semantics=("arbitrary", "arbitrary"), vmem_limit_bytes=VMEM_LIMIT),
        name="moba_attn",
    )(proj, proj, proj)


def _merge_kernel(x_ref, u_ref, v_ref, ga_ref, gb_ref, yb_ref, ws_ref, bs_ref,
                  wpa_ref, wpb_ref, wout_ref, mod_ref, o_ref, ya_scr, *, tm):
    t = GMLP_CHUNK
    r = lax.broadcasted_iota(jnp.int32, (t, t), 0)
    c = lax.broadcasted_iota(jnp.int32, (t, t), 1)
    tril = c <= r
    for g in range(GMLP_GROUPS):
        w = jnp.where(tril, ws_ref[g], 0.0).astype(BF16)
        bcol = bs_ref[:, g:g + 1]
        cols = slice(g * t, (g + 1) * t)
        for ch in range(tm // t):
            rows = slice(ch * t, (ch + 1) * t)
            mixed = _dot(w, v_ref[rows, cols]) + bcol
            ya_scr[rows, cols] = (u_ref[rows, cols].astype(F32) * mixed).astype(BF16)

    pa = _dot(ya_scr[...], wpa_ref[...])
    pb = _dot(yb_ref[...], wpb_ref[...])
    merged = ga_ref[...].astype(F32) * pa + gb_ref[...].astype(F32) * pb
    out = _dot(merged.astype(BF16), wout_ref[...])
    o_ref[...] = x_ref[...] + mod_ref[G_M:G_M + 1, :] * out


def _merge(x2d, proj, yb, w_sp, b_sp_t, wpa, wpb, wout, mod3, *, seq, tm):
    m, d = x2d.shape
    tiles_per_batch = seq // tm
    t = GMLP_CHUNK
    kern = functools.partial(_merge_kernel, tm=tm)
    const2 = lambda i: (0, 0)
    return pl.pallas_call(
        kern,
        grid=(m // tm,),
        in_specs=[
            pl.BlockSpec((tm, d), lambda i: (i, 0)),
            pl.BlockSpec((tm, d), lambda i: (i, 0)),
            pl.BlockSpec((tm, d), lambda i: (i, 1)),
            pl.BlockSpec((tm, d), lambda i: (i, 5)),
            pl.BlockSpec((tm, d), lambda i: (i, 6)),
            pl.BlockSpec((tm, d), lambda i: (i, 0)),
            pl.BlockSpec((GMLP_GROUPS, t, t), lambda i: (0, 0, 0)),
            pl.BlockSpec((t, GMLP_GROUPS), const2),
            pl.BlockSpec((d, d), const2),
            pl.BlockSpec((d, d), const2),
            pl.BlockSpec((d, d), const2),
            pl.BlockSpec((None, N_MOD, d), lambda i: (i // tiles_per_batch, 0, 0)),
        ],
        out_specs=pl.BlockSpec((tm, d), lambda i: (i, 0)),
        out_shape=jax.ShapeDtypeStruct((m, d), F32),
        scratch_shapes=[pltpu.VMEM((tm, d), BF16)],
        compiler_params=pltpu.CompilerParams(
            dimension_semantics=("arbitrary",), vmem_limit_bytes=VMEM_LIMIT),
        name="merge",
    )(x2d, proj, proj, proj, proj, yb, w_sp, b_sp_t, wpa, wpb, wout, mod3)


def _ffn_kernel(x_ref, mod_ref, g_ref, wg_ref, wu_ref, wd_ref, gfin_ref, o_ref, *, final_norm):
    x = x_ref[...]
    gain = g_ref[...] * (1.0 + mod_ref[SC_F:SC_F + 1, :])
    h = _modulated_rmsnorm(x, gain, mod_ref[SH_F:SH_F + 1, :]).astype(BF16)
    gate = _dot(h, wg_ref[...])
    up = _dot(h, wu_ref[...])
    a = (gate * jax.nn.sigmoid(gate) * up).astype(BF16)
    y = x + mod_ref[G_F:G_F + 1, :] * _dot(a, wd_ref[...])
    if final_norm:
        y = y * lax.rsqrt(jnp.mean(y * y, axis=-1, keepdims=True) + EPS) * gfin_ref[...]
    o_ref[...] = y


def _ffn(x2d, mod3, norm_g, wg, wu, wd, g_final, *, seq, tm, final_norm):
    m, d = x2d.shape
    fh = wg.shape[1]
    tiles_per_batch = seq // tm
    kern = functools.partial(_ffn_kernel, final_norm=final_norm)
    const2 = lambda i: (0, 0)
    resident = dict(pipeline_mode=pl.Buffered(1))
    return pl.pallas_call(
        kern,
        grid=(m // tm,),
        in_specs=[
            pl.BlockSpec((tm, d), lambda i: (i, 0)),
            pl.BlockSpec((None, N_MOD, d), lambda i: (i // tiles_per_batch, 0, 0)),
            pl.BlockSpec((1, d), const2),
            pl.BlockSpec((d, fh), const2, **resident),
            pl.BlockSpec((d, fh), const2, **resident),
            pl.BlockSpec((fh, d), const2, **resident),
            pl.BlockSpec((1, d), const2),
        ],
        out_specs=pl.BlockSpec((tm, d), lambda i: (i, 0)),
        out_shape=jax.ShapeDtypeStruct((m, d), F32),
        compiler_params=pltpu.CompilerParams(
            dimension_semantics=("arbitrary",), vmem_limit_bytes=VMEM_LIMIT),
        name="ffn",
    )(x2d, mod3, norm_g, wg, wu, wd, g_final)


def kernel(x, c, w_ada, b_ada, norm_mix_g, w_in, ln_v_g, ln_v_b, w_spatial, b_spatial,
           w_proj_a, w_proj_b, w_out, norm_ffn_g, w_ffn_gate, w_ffn_up, w_ffn_down,
           norm_final_g):
    batch, seq, d = x.shape
    depth = w_ada.shape[0]
    assert d == MOBA_HEADS * MOBA_HEAD_DIM == GMLP_GROUPS * LANES
    assert seq % MOBA_BLOCK == 0 and seq // MOBA_BLOCK <= LANES
    assert w_in.shape[2] == N_SECTIONS * d

    x2d = x.reshape(batch * seq, d)
    c8 = jnp.pad(c, ((0, 8 - batch), (0, 0)))
    row = lambda a: a.reshape(1, -1)
    g_final = row(norm_final_g)

    for l in range(depth):
        mod = _adaln(c8, w_ada[l], row(b_ada[l]))
        mod3 = mod[:batch].reshape(batch, N_MOD, d)
        proj = _proj(x2d, mod3, row(norm_mix_g[l]), w_in[l].astype(BF16),
                     row(ln_v_g[l]), row(ln_v_b[l]), seq=seq, tm=512)
        yb = _attention(proj, batch=batch, seq=seq, d=d)
        x2d = _merge(x2d, proj, yb, w_spatial[l], b_spatial[l].T,
                     w_proj_a[l].astype(BF16), w_proj_b[l].astype(BF16), w_out[l].astype(BF16),
                     mod3, seq=seq, tm=512)
        x2d = _ffn(x2d, mod3, row(norm_ffn_g[l]), w_ffn_gate[l].astype(BF16),
                   w_ffn_up[l].astype(BF16), w_ffn_down[l].astype(BF16), g_final,
                   seq=seq, tm=512, final_norm=(l == depth - 1))
    return x2d.reshape(batch, seq, d)
```

```python
import functools
import math

import jax
import jax.numpy as jnp
from jax import lax
from jax.experimental import pallas as pl
from jax.experimental.pallas import tpu as pltpu

F32 = jnp.float32
BF16 = jnp.bfloat16

GMLP_CHUNK = 128
GMLP_GROUPS = 8
MOBA_HEADS = 8
MOBA_HEAD_DIM = 128
MOBA_BLOCK = 256
MOBA_TOPK = 3
N_MOD = 6
N_SECTIONS = 7
EPS = 1e-6
NEG = -1e30
LANES = 128
VMEM_LIMIT = 56 * 1024 * 1024

SH_M, SC_M, G_M, SH_F, SC_F, G_F = range(N_MOD)


def _dot(a, b):
    return jnp.dot(a, b, preferred_element_type=F32)


def _dot_nt(a, b):
    return lax.dot_general(a, b, (((1,), (1,)), ((), ())), preferred_element_type=F32)


def _split_bf16(a):
    hi = a.astype(BF16)
    lo = (a - hi.astype(F32)).astype(BF16)
    return hi, lo


def _adaln_kernel(c_ref, w_ref, b_ref, o_ref):
    c = c_ref[...]
    c_act = c * jax.nn.sigmoid(c)
    c_hi, c_lo = _split_bf16(c_act)
    w_hi, w_lo = _split_bf16(w_ref[...])
    o_ref[...] = _dot(c_hi, w_hi) + (_dot(c_hi, w_lo) + _dot(c_lo, w_hi)) + b_ref[...]


def _adaln(c8, w_ada, b_ada):
    rows, d = c8.shape
    n = w_ada.shape[1]
    tn = d
    return pl.pallas_call(
        _adaln_kernel,
        grid=(n // tn,),
        in_specs=[
            pl.BlockSpec((rows, d), lambda j: (0, 0)),
            pl.BlockSpec((d, tn), lambda j: (0, j)),
            pl.BlockSpec((1, tn), lambda j: (0, j)),
        ],
        out_specs=pl.BlockSpec((rows, tn), lambda j: (0, j)),
        out_shape=jax.ShapeDtypeStruct((rows, n), F32),
        compiler_params=pltpu.CompilerParams(
            dimension_semantics=("arbitrary",), vmem_limit_bytes=VMEM_LIMIT),
        name="adaln",
    )(c8, w_ada, b_ada)


def _modulated_rmsnorm(x, gain_row, shift_row):
    r = lax.rsqrt(jnp.mean(x * x, axis=-1, keepdims=True) + EPS)
    return x * r * gain_row + shift_row


def _proj_kernel(x_ref, mod_ref, g_ref, w_ref, lng_ref, lnb_ref, o_ref, h_scr, *, q_scale):
    j = pl.program_id(1)

    @pl.when(j == 0)
    def _():
        gain = g_ref[...] * (1.0 + mod_ref[SC_M:SC_M + 1, :])
        h = _modulated_rmsnorm(x_ref[...], gain, mod_ref[SH_M:SH_M + 1, :])
        h_scr[...] = h.astype(BF16)

    acc = _dot(h_scr[...], w_ref[...])

    @pl.when(j == 0)
    def _():
        o_ref[...] = jax.nn.gelu(acc).astype(BF16)

    @pl.when(j == 1)
    def _():
        v = jax.nn.gelu(acc)
        mu = jnp.mean(v, axis=-1, keepdims=True)
        vc = v - mu
        var = jnp.mean(vc * vc, axis=-1, keepdims=True)
        o_ref[...] = (vc * lax.rsqrt(var + EPS) * lng_ref[...] + lnb_ref[...]).astype(BF16)

    @pl.when(j == 2)
    def _():
        o_ref[...] = (acc * q_scale).astype(BF16)

    @pl.when((j == 3) | (j == 4))
    def _():
        o_ref[...] = acc.astype(BF16)

    @pl.when(j >= 5)
    def _():
        o_ref[...] = jax.nn.sigmoid(acc).astype(BF16)


def _proj(x2d, mod3, norm_g, w_in_bf, ln_g, ln_b, *, seq, tm):
    m, d = x2d.shape
    tiles_per_batch = seq // tm
    kern = functools.partial(_proj_kernel, q_scale=MOBA_HEAD_DIM ** -0.5 * math.log2(math.e))
    return pl.pallas_call(
        kern,
        grid=(m // tm, N_SECTIONS),
        in_specs=[
            pl.BlockSpec((tm, d), lambda i, j: (i, 0)),
            pl.BlockSpec((None, N_MOD, d), lambda i, j: (i // tiles_per_batch, 0, 0)),
            pl.BlockSpec((1, d), lambda i, j: (0, 0)),
            pl.BlockSpec((d, d), lambda i, j: (0, j)),
            pl.BlockSpec((1, d), lambda i, j: (0, 0)),
            pl.BlockSpec((1, d), lambda i, j: (0, 0)),
        ],
        out_specs=pl.BlockSpec((tm, d), lambda i, j: (i, j)),
        out_shape=jax.ShapeDtypeStruct((m, N_SECTIONS * d), BF16),
        scratch_shapes=[pltpu.VMEM((tm, d), BF16)],
        compiler_params=pltpu.CompilerParams(
            dimension_semantics=("arbitrary", "arbitrary"), vmem_limit_bytes=VMEM_LIMIT),
        name="proj",
    )(x2d, mod3, norm_g, w_in_bf, ln_g, ln_b)


def _attn_kernel(q_ref, k_ref, v_ref, o_ref, kbar_scr, vt_scr, sel_scr, s_scr, p_scr,
                 m_scr, l_scr, acc_scr, *, n_blocks, heads, tq):
    blk = MOBA_BLOCK
    hd = MOBA_HEAD_DIM
    qpb = tq // blk
    hcols = [slice(h * hd, (h + 1) * hd) for h in range(heads)]

    for b in range(n_blocks):
        rows = slice(b * blk, (b + 1) * blk)
        kbar_scr[b:b + 1, :] = jnp.sum(k_ref[rows, :].astype(F32), axis=0, keepdims=True) * (1.0 / blk)
        for h in range(heads):
            vt_scr[b, hcols[h], :] = v_ref[rows, hcols[h]].astype(F32).T.astype(BF16)

    blk_id = lax.broadcasted_iota(jnp.int32, (n_blocks, tq), 0).astype(F32)
    col_blk = (lax.broadcasted_iota(jnp.int32, (1, tq), 1) // blk).astype(F32)
    key_pos = lax.broadcasted_iota(jnp.int32, (blk, tq), 0)
    qry_pos = lax.broadcasted_iota(jnp.int32, (blk, tq), 1)

    def q_tile(qt, carry):
        qrows = pl.ds(pl.multiple_of(qt * tq, tq), tq)
        first_blk = qt * qpb
        own_blk = first_blk.astype(F32) + col_blk

        for h in range(heads):
            q = q_ref[qrows, hcols[h]]
            kb_hi, kb_lo = _split_bf16(kbar_scr[:, hcols[h]])
            gate = _dot_nt(kb_hi, q) + _dot_nt(kb_lo, q)
            g = jnp.where(blk_id < own_blk, gate, NEG)
            sel_bias = jnp.where(blk_id == own_blk, 0.0, NEG)
            for _ in range(MOBA_TOPK):
                gmax = jnp.max(g, axis=0, keepdims=True)
                idx = jnp.min(jnp.where(g == gmax, blk_id, float(n_blocks)), axis=0, keepdims=True)
                pick = blk_id == idx
                sel_bias = jnp.where(pick & (idx < own_blk), 0.0, sel_bias)
                g = jnp.where(pick, -jnp.inf, g)
            sel_scr[h] = sel_bias
            m_scr[h] = jnp.full((1, tq), -jnp.inf, F32)
            l_scr[h] = jnp.zeros((1, tq), F32)
            acc_scr[h] = jnp.zeros((hd, tq), F32)

        def step(kb, causal_off):
            krows = pl.ds(pl.multiple_of(kb * blk, blk), blk)
            cmax = []
            for h in range(heads):
                s = _dot_nt(k_ref[krows, hcols[h]], q_ref[qrows, hcols[h]]) + sel_scr[h, pl.ds(kb, 1), :]
                if causal_off is not None:
                    s = jnp.where(key_pos + causal_off <= qry_pos, s, NEG)
                s_scr[h] = s
                cmax.append(jnp.max(s, axis=0, keepdims=True))
            alphas = []
            for h in range(heads):
                m = m_scr[h]
                m_new = jnp.maximum(m, cmax[h])
                alpha = jnp.exp2(m - m_new)
                p = jnp.exp2(s_scr[h] - m_new)
                p_scr[h] = p.astype(BF16)
                l_scr[h] = alpha * l_scr[h] + jnp.sum(p, axis=0, keepdims=True)
                m_scr[h] = m_new
                alphas.append(alpha)
            for h in range(heads):
                acc_scr[h] = alphas[h] * acc_scr[h] + _dot(vt_scr[kb, hcols[h], :], p_scr[h])

        def past_step(kb, c):
            step(kb, None)
            return c

        lax.fori_loop(0, first_blk, past_step, 0)
        for d in range(qpb):
            step(first_blk + d, d * blk)

        for h in range(heads):
            o_ref[qrows, hcols[h]] = (acc_scr[h] / l_scr[h]).T.astype(BF16)
        return carry

    lax.fori_loop(0, n_blocks // qpb, q_tile, 0)


ATTN_HEADS_PER_STEP = 4
ATTN_Q_TILE = 512


def _attention(proj, *, batch, seq, d):
    hd = MOBA_HEAD_DIM
    blk = MOBA_BLOCK
    n_blocks = seq // blk
    hg = ATTN_HEADS_PER_STEP
    tq = ATTN_Q_TILE
    groups = MOBA_HEADS // hg
    q_off, k_off, v_off = 2 * groups, 3 * groups, 4 * groups
    kern = functools.partial(_attn_kernel, n_blocks=n_blocks, heads=hg, tq=tq)
    return pl.pallas_call(
        kern,
        grid=(batch, groups),
        in_specs=[
            pl.BlockSpec((seq, hg * hd), lambda b, g: (b, q_off + g)),
            pl.BlockSpec((seq, hg * hd), lambda b, g: (b, k_off + g)),
            pl.BlockSpec((seq, hg * hd), lambda b, g: (b, v_off + g)),
        ],
        out_specs=pl.BlockSpec((seq, hg * hd), lambda b, g: (b, g)),
        out_shape=jax.ShapeDtypeStruct((batch * seq, MOBA_HEADS * hd), BF16),
        scratch_shapes=[
            pltpu.VMEM((n_blocks, hg * hd), F32),
            pltpu.VMEM((n_blocks, hg * hd, blk), BF16),
            pltpu.VMEM((hg, n_blocks, tq), F32),
            pltpu.VMEM((hg, blk, tq), F32),
            pltpu.VMEM((hg, blk, tq), BF16),
            pltpu.VMEM((hg, 1, tq), F32),
            pltpu.VMEM((hg, 1, tq), F32),
            pltpu.VMEM((hg, hd, tq), F32),
        ],
        compiler_params=pltpu.CompilerParams(
            dimension_semantics=("arbitrary", "arbitrary"), vmem_limit_bytes=VMEM_LIMIT),
        name="moba_attn",
    )(proj, proj, proj)


def _merge_kernel(x_ref, u_ref, v_ref, ga_ref, gb_ref, yb_ref, ws_ref, bs_ref,
                  wpa_ref, wpb_ref, wout_ref, mod_ref, o_ref, ya_scr, *, tm):
    t = GMLP_CHUNK
    r = lax.broadcasted_iota(jnp.int32, (t, t), 0)
    c = lax.broadcasted_iota(jnp.int32, (t, t), 1)
    tril = c <= r
    for g in range(GMLP_GROUPS):
        w = jnp.where(tril, ws_ref[g], 0.0).astype(BF16)
        bcol = bs_ref[:, g:g + 1]
        cols = slice(g * t, (g + 1) * t)
        for ch in range(tm // t):
            rows = slice(ch * t, (ch + 1) * t)
            mixed = _dot(w, v_ref[rows, cols]) + bcol
            ya_scr[rows, cols] = (u_ref[rows, cols].astype(F32) * mixed).astype(BF16)

    pa = _dot(ya_scr[...], wpa_ref[...])
    pb = _dot(yb_ref[...], wpb_ref[...])
    merged = ga_ref[...].astype(F32) * pa + gb_ref[...].astype(F32) * pb
    out = _dot(merged.astype(BF16), wout_ref[...])
    o_ref[...] = x_ref[...] + mod_ref[G_M:G_M + 1, :] * out


def _merge(x2d, proj, yb, w_sp, b_sp_t, wpa, wpb, wout, mod3, *, seq, tm):
    m, d = x2d.shape
    tiles_per_batch = seq // tm
    t = GMLP_CHUNK
    kern = functools.partial(_merge_kernel, tm=tm)
    const2 = lambda i: (0, 0)
    return pl.pallas_call(
        kern,
        grid=(m // tm,),
        in_specs=[
            pl.BlockSpec((tm, d), lambda i: (i, 0)),
            pl.BlockSpec((tm, d), lambda i: (i, 0)),
            pl.BlockSpec((tm, d), lambda i: (i, 1)),
            pl.BlockSpec((tm, d), lambda i: (i, 5)),
            pl.BlockSpec((tm, d), lambda i: (i, 6)),
            pl.BlockSpec((tm, d), lambda i: (i, 0)),
            pl.BlockSpec((GMLP_GROUPS, t, t), lambda i: (0, 0, 0)),
            pl.BlockSpec((t, GMLP_GROUPS), const2),
            pl.BlockSpec((d, d), const2),
            pl.BlockSpec((d, d), const2),
            pl.BlockSpec((d, d), const2),
            pl.BlockSpec((None, N_MOD, d), lambda i: (i // tiles_per_batch, 0, 0)),
        ],
        out_specs=pl.BlockSpec((tm, d), lambda i: (i, 0)),
        out_shape=jax.ShapeDtypeStruct((m, d), F32),
        scratch_shapes=[pltpu.VMEM((tm, d), BF16)],
        compiler_params=pltpu.CompilerParams(
            dimension_semantics=("arbitrary",), vmem_limit_bytes=VMEM_LIMIT),
        name="merge",
    )(x2d, proj, proj, proj, proj, yb, w_sp, b_sp_t, wpa, wpb, wout, mod3)


def _ffn_kernel(x_ref, mod_ref, g_ref, wg_ref, wu_ref, wd_ref, gfin_ref, o_ref, *, final_norm):
    x = x_ref[...]
    gain = g_ref[...] * (1.0 + mod_ref[SC_F:SC_F + 1, :])
    h = _modulated_rmsnorm(x, gain, mod_ref[SH_F:SH_F + 1, :]).astype(BF16)
    gate = _dot(h, wg_ref[...])
    up = _dot(h, wu_ref[...])
    a = (gate * jax.nn.sigmoid(gate) * up).astype(BF16)
    y = x + mod_ref[G_F:G_F + 1, :] * _dot(a, wd_ref[...])
    if final_norm:
        y = y * lax.rsqrt(jnp.mean(y * y, axis=-1, keepdims=True) + EPS) * gfin_ref[...]
    o_ref[...] = y


def _ffn(x2d, mod3, norm_g, wg, wu, wd, g_final, *, seq, tm, final_norm):
    m, d = x2d.shape
    fh = wg.shape[1]
    tiles_per_batch = seq // tm
    kern = functools.partial(_ffn_kernel, final_norm=final_norm)
    const2 = lambda i: (0, 0)
    resident = dict(pipeline_mode=pl.Buffered(1))
    return pl.pallas_call(
        kern,
        grid=(m // tm,),
        in_specs=[
            pl.BlockSpec((tm, d), lambda i: (i, 0)),
            pl.BlockSpec((None, N_MOD, d), lambda i: (i // tiles_per_batch, 0, 0)),
            pl.BlockSpec((1, d), const2),
            pl.BlockSpec((d, fh), const2, **resident),
            pl.BlockSpec((d, fh), const2, **resident),
            pl.BlockSpec((fh, d), const2, **resident),
            pl.BlockSpec((1, d), const2),
        ],
        out_specs=pl.BlockSpec((tm, d), lambda i: (i, 0)),
        out_shape=jax.ShapeDtypeStruct((m, d), F32),
        compiler_params=pltpu.CompilerParams(
            dimension_semantics=("arbitrary",), vmem_limit_bytes=VMEM_LIMIT),
        name="ffn",
    )(x2d, mod3, norm_g, wg, wu, wd, g_final)


def kernel(x, c, w_ada, b_ada, norm_mix_g, w_in, ln_v_g, ln_v_b, w_spatial, b_spatial,
           w_proj_a, w_proj_b, w_out, norm_ffn_g, w_ffn_gate, w_ffn_up, w_ffn_down,
           norm_final_g):
    batch, seq, d = x.shape
    depth = w_ada.shape[0]
    assert d == MOBA_HEADS * MOBA_HEAD_DIM == GMLP_GROUPS * LANES
    assert seq % ATTN_Q_TILE == 0 and ATTN_Q_TILE % MOBA_BLOCK == 0
    assert w_in.shape[2] == N_SECTIONS * d

    x2d = x.reshape(batch * seq, d)
    c8 = jnp.pad(c, ((0, 8 - batch), (0, 0)))
    row = lambda a: a.reshape(1, -1)
    g_final = row(norm_final_g)

    for l in range(depth):
        mod = _adaln(c8, w_ada[l], row(b_ada[l]))
        mod3 = mod[:batch].reshape(batch, N_MOD, d)
        proj = _proj(x2d, mod3, row(norm_mix_g[l]), w_in[l].astype(BF16),
                     row(ln_v_g[l]), row(ln_v_b[l]), seq=seq, tm=512)
        yb = _attention(proj, batch=batch, seq=seq, d=d)
        x2d = _merge(x2d, proj, yb, w_spatial[l], b_spatial[l].T,
                     w_proj_a[l].astype(BF16), w_proj_b[l].astype(BF16), w_out[l].astype(BF16),
                     mod3, seq=seq, tm=512)
        x2d = _ffn(x2d, mod3, row(norm_ffn_g[l]), w_ffn_gate[l].astype(BF16),
                   w_ffn_up[l].astype(BF16), w_ffn_down[l].astype(BF16), g_final,
                   seq=seq, tm=512, final_norm=(l == depth - 1))
    return x2d.reshape(batch, seq, d)
```

```python
import functools
import math

import jax
import jax.numpy as jnp
from jax import lax
from jax.experimental import pallas as pl
from jax.experimental.pallas import tpu as pltpu

F32 = jnp.float32
BF16 = jnp.bfloat16

GMLP_CHUNK = 128
GMLP_GROUPS = 8
MOBA_HEADS = 8
MOBA_HEAD_DIM = 128
MOBA_BLOCK = 256
MOBA_TOPK = 3
N_MOD = 6
N_SECTIONS = 7
EPS = 1e-6
NEG = -1e30
LANES = 128
VMEM_LIMIT = 56 * 1024 * 1024

SH_M, SC_M, G_M, SH_F, SC_F, G_F = range(N_MOD)


def _dot(a, b):
    return jnp.dot(a, b, preferred_element_type=F32)


def _dot_nt(a, b):
    return lax.dot_general(a, b, (((1,), (1,)), ((), ())), preferred_element_type=F32)


def _split_bf16(a):
    hi = a.astype(BF16)
    lo = (a - hi.astype(F32)).astype(BF16)
    return hi, lo


def _adaln_kernel(c_ref, w_ref, b_ref, o_ref):
    c = c_ref[...]
    c_act = c * jax.nn.sigmoid(c)
    c_hi, c_lo = _split_bf16(c_act)
    w_hi, w_lo = _split_bf16(w_ref[...])
    o_ref[...] = _dot(c_hi, w_hi) + (_dot(c_hi, w_lo) + _dot(c_lo, w_hi)) + b_ref[...]


def _adaln(c8, w_ada, b_ada):
    rows, d = c8.shape
    n = w_ada.shape[1]
    tn = d
    return pl.pallas_call(
        _adaln_kernel,
        grid=(n // tn,),
        in_specs=[
            pl.BlockSpec((rows, d), lambda j: (0, 0)),
            pl.BlockSpec((d, tn), lambda j: (0, j)),
            pl.BlockSpec((1, tn), lambda j: (0, j)),
        ],
        out_specs=pl.BlockSpec((rows, tn), lambda j: (0, j)),
        out_shape=jax.ShapeDtypeStruct((rows, n), F32),
        compiler_params=pltpu.CompilerParams(
            dimension_semantics=("arbitrary",), vmem_limit_bytes=VMEM_LIMIT),
        name="adaln",
    )(c8, w_ada, b_ada)


PROJ_ROW_CHUNK = 256


def _modulated_rmsnorm(x, gain_row, shift_row):
    r = lax.rsqrt(jnp.mean(x * x, axis=-1, keepdims=True) + EPS)
    return x * r * gain_row + shift_row


def _proj_kernel(x_ref, mod_ref, g_ref, w_ref, lng_ref, lnb_ref, o_ref, h_scr, *, q_scale):
    j = pl.program_id(1)

    @pl.when(j == 0)
    def _():
        gain = g_ref[...] * (1.0 + mod_ref[SC_M:SC_M + 1, :])
        h = _modulated_rmsnorm(x_ref[...], gain, mod_ref[SH_M:SH_M + 1, :])
        h_scr[...] = h.astype(BF16)

    def section(epilogue):
        for r in range(0, h_scr.shape[0], PROJ_ROW_CHUNK):
            rows = slice(r, r + PROJ_ROW_CHUNK)
            o_ref[rows, :] = epilogue(_dot(h_scr[rows, :], w_ref[...])).astype(BF16)

    def gelu_layernorm(acc):
        v = jax.nn.gelu(acc)
        mu = jnp.mean(v, axis=-1, keepdims=True)
        vc = v - mu
        var = jnp.mean(vc * vc, axis=-1, keepdims=True)
        return vc * lax.rsqrt(var + EPS) * lng_ref[...] + lnb_ref[...]

    pl.when(j == 0)(lambda: section(jax.nn.gelu))
    pl.when(j == 1)(lambda: section(gelu_layernorm))
    pl.when(j == 2)(lambda: section(lambda acc: acc * q_scale))
    pl.when((j == 3) | (j == 4))(lambda: section(lambda acc: acc))
    pl.when(j >= 5)(lambda: section(jax.nn.sigmoid))


def _proj(x2d, mod3, norm_g, w_in_bf, ln_g, ln_b, *, seq, tm):
    m, d = x2d.shape
    tiles_per_batch = seq // tm
    kern = functools.partial(_proj_kernel, q_scale=MOBA_HEAD_DIM ** -0.5 * math.log2(math.e))
    return pl.pallas_call(
        kern,
        grid=(m // tm, N_SECTIONS),
        in_specs=[
            pl.BlockSpec((tm, d), lambda i, j: (i, 0)),
            pl.BlockSpec((None, N_MOD, d), lambda i, j: (i // tiles_per_batch, 0, 0)),
            pl.BlockSpec((1, d), lambda i, j: (0, 0)),
            pl.BlockSpec((d, d), lambda i, j: (0, j)),
            pl.BlockSpec((1, d), lambda i, j: (0, 0)),
            pl.BlockSpec((1, d), lambda i, j: (0, 0)),
        ],
        out_specs=pl.BlockSpec((tm, d), lambda i, j: (i, j)),
        out_shape=jax.ShapeDtypeStruct((m, N_SECTIONS * d), BF16),
        scratch_shapes=[pltpu.VMEM((tm, d), BF16)],
        compiler_params=pltpu.CompilerParams(
            dimension_semantics=("arbitrary", "arbitrary"), vmem_limit_bytes=VMEM_LIMIT),
        name="proj",
    )(x2d, mod3, norm_g, w_in_bf, ln_g, ln_b)


ONES_ROWS = 16


def _attn_kernel(q_ref, k_ref, v_ref, o_ref, kbar_scr, vt_scr, sel_scr, s_scr, p_scr,
                 m_scr, alpha_scr, acc_scr, *, n_blocks, heads, tq):
    blk = MOBA_BLOCK
    hd = MOBA_HEAD_DIM
    qpb = tq // blk
    assert qpb == 2
    hcols = [slice(h * hd, (h + 1) * hd) for h in range(heads)]

    ones_rows = (lax.broadcasted_iota(jnp.int32, (ONES_ROWS, blk), 0) == 0).astype(F32).astype(BF16)
    for b in range(n_blocks):
        rows = slice(b * blk, (b + 1) * blk)
        kbar_scr[b:b + 1, :] = jnp.sum(k_ref[rows, :].astype(F32), axis=0, keepdims=True) * (1.0 / blk)
        for h in range(heads):
            vt_scr[b, h, :hd, :] = v_ref[rows, hcols[h]].astype(F32).T.astype(BF16)
            vt_scr[b, h, hd:, :] = ones_rows

    blk_id = lax.broadcasted_iota(jnp.int32, (n_blocks, tq), 0).astype(F32)
    col_blk = (lax.broadcasted_iota(jnp.int32, (1, tq), 1) // blk).astype(F32)
    key_pos = lax.broadcasted_iota(jnp.int32, (blk, tq), 0)
    qry_pos = lax.broadcasted_iota(jnp.int32, (blk, tq), 1)

    def q_tile(qt, carry):
        qrows = pl.ds(pl.multiple_of(qt * tq, tq), tq)
        first_blk = qt * qpb
        own_blk = lax.convert_element_type(first_blk, F32) + col_blk

        for h in range(heads):
            q = q_ref[qrows, hcols[h]]
            kb_hi, kb_lo = _split_bf16(kbar_scr[:, hcols[h]])
            gate = _dot_nt(kb_hi, q) + _dot_nt(kb_lo, q)
            g = jnp.where(blk_id < own_blk, gate, NEG)
            sel_bias = jnp.where(blk_id == own_blk, 0.0, NEG)
            for _ in range(MOBA_TOPK):
                gmax = jnp.max(g, axis=0, keepdims=True)
                idx = jnp.min(jnp.where(g == gmax, blk_id, float(n_blocks)), axis=0, keepdims=True)
                pick = blk_id == idx
                sel_bias = jnp.where(pick & (idx < own_blk), 0.0, sel_bias)
                g = jnp.where(pick, -jnp.inf, g)
            sel_scr[h] = sel_bias
            m_scr[h] = jnp.full((1, tq), -jnp.inf, F32)
            acc_scr[h] = jnp.zeros((hd + ONES_ROWS, tq), F32)

        def scores(kb, causal_off, slot):
            krows = pl.ds(pl.multiple_of(kb * blk, blk), blk)
            cmax = []
            for h in range(heads):
                s = _dot_nt(k_ref[krows, hcols[h]], q_ref[qrows, hcols[h]]) + sel_scr[h, pl.ds(kb, 1), :]
                if causal_off is not None:
                    s = jnp.where(key_pos + causal_off <= qry_pos, s, NEG)
                s_scr[h] = s
                cmax.append(jnp.max(s, axis=0, keepdims=True))
            for h in range(heads):
                m = m_scr[h]
                m_new = jnp.maximum(m, cmax[h])
                alpha_scr[slot, h] = jnp.exp2(m - m_new)
                p_scr[slot, h] = jnp.exp2(s_scr[h] - m_new).astype(BF16)
                m_scr[h] = m_new

        def accumulate(kb, slot):
            for h in range(heads):
                acc_scr[h] = alpha_scr[slot, h] * acc_scr[h] + _dot(vt_scr[kb, h], p_scr[slot, h])

        scores(first_blk, 0, 0)
        scores(first_blk + 1, blk, 1)
        accumulate(first_blk, 0)

        def past_pair(i, c):
            kb0 = 2 * i
            prev = jnp.where(i == 0, first_blk + 1, kb0 - 1)
            scores(kb0, None, 0)
            accumulate(prev, 1)
            scores(kb0 + 1, None, 1)
            accumulate(kb0, 0)
            return c

        lax.fori_loop(0, qt, past_pair, 0)
        accumulate(jnp.where(qt == 0, first_blk + 1, first_blk - 1), 1)

        for h in range(heads):
            acc = acc_scr[h]
            o_ref[qrows, hcols[h]] = (acc[:hd] / acc[hd:hd + 1]).T.astype(BF16)
        return carry

    lax.fori_loop(0, n_blocks // qpb, q_tile, 0)


ATTN_HEADS_PER_STEP = 4
ATTN_Q_TILE = 512


def _attention(proj, *, batch, seq, d):
    hd = MOBA_HEAD_DIM
    blk = MOBA_BLOCK
    n_blocks = seq // blk
    hg = ATTN_HEADS_PER_STEP
    tq = ATTN_Q_TILE
    groups = MOBA_HEADS // hg
    q_off, k_off, v_off = 2 * groups, 3 * groups, 4 * groups
    kern = functools.partial(_attn_kernel, n_blocks=n_blocks, heads=hg, tq=tq)
    return pl.pallas_call(
        kern,
        grid=(batch, groups),
        in_specs=[
            pl.BlockSpec((seq, hg * hd), lambda b, g: (b, q_off + g)),
            pl.BlockSpec((seq, hg * hd), lambda b, g: (b, k_off + g)),
            pl.BlockSpec((seq, hg * hd), lambda b, g: (b, v_off + g)),
        ],
        out_specs=pl.BlockSpec((seq, hg * hd), lambda b, g: (b, g)),
        out_shape=jax.ShapeDtypeStruct((batch * seq, MOBA_HEADS * hd), BF16),
        scratch_shapes=[
            pltpu.VMEM((n_blocks, hg * hd), F32),
            pltpu.VMEM((n_blocks, hg, hd + ONES_ROWS, blk), BF16),
            pltpu.VMEM((hg, n_blocks, tq), F32),
            pltpu.VMEM((hg, blk, tq), F32),
            pltpu.VMEM((2, hg, blk, tq), BF16),
            pltpu.VMEM((hg, 1, tq), F32),
            pltpu.VMEM((2, hg, 1, tq), F32),
            pltpu.VMEM((hg, hd + ONES_ROWS, tq), F32),
        ],
        compiler_params=pltpu.CompilerParams(
            dimension_semantics=("arbitrary", "arbitrary"), vmem_limit_bytes=VMEM_LIMIT),
        name="moba_attn",
    )(proj, proj, proj)


def _merge_kernel(x_ref, u_ref, v_ref, ga_ref, gb_ref, yb_ref, ws_ref, bs_ref,
                  wpa_ref, wpb_ref, wout_ref, mod_ref, o_ref, ya_scr, *, tm):
    t = GMLP_CHUNK
    r = lax.broadcasted_iota(jnp.int32, (t, t), 0)
    c = lax.broadcasted_iota(jnp.int32, (t, t), 1)
    tril = c <= r
    for g in range(GMLP_GROUPS):
        w = jnp.where(tril, ws_ref[g], 0.0).astype(BF16)
        bcol = bs_ref[:, g:g + 1]
        cols = slice(g * t, (g + 1) * t)
        for ch in range(tm // t):
            rows = slice(ch * t, (ch + 1) * t)
            mixed = _dot(w, v_ref[rows, cols]) + bcol
            ya_scr[rows, cols] = (u_ref[rows, cols].astype(F32) * mixed).astype(BF16)

    pa = _dot(ya_scr[...], wpa_ref[...])
    pb = _dot(yb_ref[...], wpb_ref[...])
    merged = ga_ref[...].astype(F32) * pa + gb_ref[...].astype(F32) * pb
    out = _dot(merged.astype(BF16), wout_ref[...])
    o_ref[...] = x_ref[...] + mod_ref[G_M:G_M + 1, :] * out


def _merge(x2d, proj, yb, w_sp, b_sp_t, wpa, wpb, wout, mod3, *, seq, tm):
    m, d = x2d.shape
    tiles_per_batch = seq // tm
    t = GMLP_CHUNK
    kern = functools.partial(_merge_kernel, tm=tm)
    const2 = lambda i: (0, 0)
    return pl.pallas_call(
        kern,
        grid=(m // tm,),
        in_specs=[
            pl.BlockSpec((tm, d), lambda i: (i, 0)),
            pl.BlockSpec((tm, d), lambda i: (i, 0)),
            pl.BlockSpec((tm, d), lambda i: (i, 1)),
            pl.BlockSpec((tm, d), lambda i: (i, 5)),
            pl.BlockSpec((tm, d), lambda i: (i, 6)),
            pl.BlockSpec((tm, d), lambda i: (i, 0)),
            pl.BlockSpec((GMLP_GROUPS, t, t), lambda i: (0, 0, 0)),
            pl.BlockSpec((t, GMLP_GROUPS), const2),
            pl.BlockSpec((d, d), const2),
            pl.BlockSpec((d, d), const2),
            pl.BlockSpec((d, d), const2),
            pl.BlockSpec((None, N_MOD, d), lambda i: (i // tiles_per_batch, 0, 0)),
        ],
        out_specs=pl.BlockSpec((tm, d), lambda i: (i, 0)),
        out_shape=jax.ShapeDtypeStruct((m, d), F32),
        scratch_shapes=[pltpu.VMEM((tm, d), BF16)],
        compiler_params=pltpu.CompilerParams(
            dimension_semantics=("arbitrary",), vmem_limit_bytes=VMEM_LIMIT),
        name="merge",
    )(x2d, proj, proj, proj, proj, yb, w_sp, b_sp_t, wpa, wpb, wout, mod3)


def _ffn_kernel(x_ref, mod_ref, g_ref, wg_ref, wu_ref, wd_ref, gfin_ref, o_ref, *, final_norm):
    x = x_ref[...]
    gain = g_ref[...] * (1.0 + mod_ref[SC_F:SC_F + 1, :])
    h = _modulated_rmsnorm(x, gain, mod_ref[SH_F:SH_F + 1, :]).astype(BF16)
    gate = _dot(h, wg_ref[...])
    up = _dot(h, wu_ref[...])
    a = (gate * jax.nn.sigmoid(gate) * up).astype(BF16)
    y = x + mod_ref[G_F:G_F + 1, :] * _dot(a, wd_ref[...])
    if final_norm:
        y = y * lax.rsqrt(jnp.mean(y * y, axis=-1, keepdims=True) + EPS) * gfin_ref[...]
    o_ref[...] = y


def _ffn(x2d, mod3, norm_g, wg, wu, wd, g_final, *, seq, tm, final_norm):
    m, d = x2d.shape
    fh = wg.shape[1]
    tiles_per_batch = seq // tm
    kern = functools.partial(_ffn_kernel, final_norm=final_norm)
    const2 = lambda i: (0, 0)
    resident = dict(pipeline_mode=pl.Buffered(1))
    return pl.pallas_call(
        kern,
        grid=(m // tm,),
        in_specs=[
            pl.BlockSpec((tm, d), lambda i: (i, 0)),
            pl.BlockSpec((None, N_MOD, d), lambda i: (i // tiles_per_batch, 0, 0)),
            pl.BlockSpec((1, d), const2),
            pl.BlockSpec((d, fh), const2, **resident),
            pl.BlockSpec((d, fh), const2, **resident),
            pl.BlockSpec((fh, d), const2, **resident),
            pl.BlockSpec((1, d), const2),
        ],
        out_specs=pl.BlockSpec((tm, d), lambda i: (i, 0)),
        out_shape=jax.ShapeDtypeStruct((m, d), F32),
        compiler_params=pltpu.CompilerParams(
            dimension_semantics=("arbitrary",), vmem_limit_bytes=VMEM_LIMIT),
        name="ffn",
    )(x2d, mod3, norm_g, wg, wu, wd, g_final)


def kernel(x, c, w_ada, b_ada, norm_mix_g, w_in, ln_v_g, ln_v_b, w_spatial, b_spatial,
           w_proj_a, w_proj_b, w_out, norm_ffn_g, w_ffn_gate, w_ffn_up, w_ffn_down,
           norm_final_g):
    batch, seq, d = x.shape
    depth = w_ada.shape[0]
    assert d == MOBA_HEADS * MOBA_HEAD_DIM == GMLP_GROUPS * LANES
    assert seq % ATTN_Q_TILE == 0 and ATTN_Q_TILE % MOBA_BLOCK == 0
    assert w_in.shape[2] == N_SECTIONS * d

    x2d = x.reshape(batch * seq, d)
    c8 = jnp.pad(c, ((0, 8 - batch), (0, 0)))
    row = lambda a: a.reshape(1, -1)
    g_final = row(norm_final_g)

    for l in range(depth):
        mod = _adaln(c8, w_ada[l], row(b_ada[l]))
        mod3 = mod[:batch].reshape(batch, N_MOD, d)
        proj = _proj(x2d, mod3, row(norm_mix_g[l]), w_in[l].astype(BF16),
                     row(ln_v_g[l]), row(ln_v_b[l]), seq=seq, tm=1024)
        yb = _attention(proj, batch=batch, seq=seq, d=d)
        x2d = _merge(x2d, proj, yb, w_spatial[l], b_spatial[l].T,
                     w_proj_a[l].astype(BF16), w_proj_b[l].astype(BF16), w_out[l].astype(BF16),
                     mod3, seq=seq, tm=512)
        x2d = _ffn(x2d, mod3, row(norm_ffn_g[l]), w_ffn_gate[l].astype(BF16),
                   w_ffn_up[l].astype(BF16), w_ffn_down[l].astype(BF16), g_final,
                   seq=seq, tm=512, final_norm=(l == depth - 1))
    return x2d.reshape(batch, seq, d)
```

```python
import functools
import math

import jax
import jax.numpy as jnp
from jax import lax
from jax.experimental import pallas as pl
from jax.experimental.pallas import tpu as pltpu

F32 = jnp.float32
BF16 = jnp.bfloat16

GMLP_CHUNK = 128
GMLP_GROUPS = 8
MOBA_HEADS = 8
MOBA_HEAD_DIM = 128
MOBA_BLOCK = 256
MOBA_TOPK = 3
N_MOD = 6
N_SECTIONS = 7
EPS = 1e-6
NEG = -1e30
LANES = 128
VMEM_LIMIT = 56 * 1024 * 1024

SH_M, SC_M, G_M, SH_F, SC_F, G_F = range(N_MOD)


def _dot(a, b):
    return jnp.dot(a, b, preferred_element_type=F32)


def _dot_nt(a, b):
    return lax.dot_general(a, b, (((1,), (1,)), ((), ())), preferred_element_type=F32)


def _split_bf16(a):
    hi = a.astype(BF16)
    lo = (a - hi.astype(F32)).astype(BF16)
    return hi, lo


def _adaln_kernel(c_ref, w_ref, b_ref, o_ref):
    c = c_ref[...]
    c_act = c * jax.nn.sigmoid(c)
    c_hi, c_lo = _split_bf16(c_act)
    w_hi, w_lo = _split_bf16(w_ref[...])
    o_ref[...] = _dot(c_hi, w_hi) + (_dot(c_hi, w_lo) + _dot(c_lo, w_hi)) + b_ref[...]


def _adaln(c8, w_ada, b_ada):
    rows, d = c8.shape
    n = w_ada.shape[1]
    tn = d
    return pl.pallas_call(
        _adaln_kernel,
        grid=(n // tn,),
        in_specs=[
            pl.BlockSpec((rows, d), lambda j: (0, 0)),
            pl.BlockSpec((d, tn), lambda j: (0, j)),
            pl.BlockSpec((1, tn), lambda j: (0, j)),
        ],
        out_specs=pl.BlockSpec((rows, tn), lambda j: (0, j)),
        out_shape=jax.ShapeDtypeStruct((rows, n), F32),
        compiler_params=pltpu.CompilerParams(
            dimension_semantics=("arbitrary",), vmem_limit_bytes=VMEM_LIMIT),
        name="adaln",
    )(c8, w_ada, b_ada)


PROJ_ROW_CHUNK = 256


def _modulated_rmsnorm(x, gain_row, shift_row):
    r = lax.rsqrt(jnp.mean(x * x, axis=-1, keepdims=True) + EPS)
    return x * r * gain_row + shift_row


def _proj_kernel(x_ref, mod_ref, g_ref, w_ref, lng_ref, lnb_ref, o_ref, *, q_scale):
    d = x_ref.shape[1]
    gain = g_ref[...] * (1.0 + mod_ref[SC_M:SC_M + 1, :])
    h = _modulated_rmsnorm(x_ref[...], gain, mod_ref[SH_M:SH_M + 1, :]).astype(BF16)

    def gelu_layernorm(acc):
        v = jax.nn.gelu(acc)
        mu = jnp.mean(v, axis=-1, keepdims=True)
        vc = v - mu
        var = jnp.mean(vc * vc, axis=-1, keepdims=True)
        return vc * lax.rsqrt(var + EPS) * lng_ref[...] + lnb_ref[...]

    identity = lambda acc: acc
    epilogues = (jax.nn.gelu, gelu_layernorm, lambda acc: acc * q_scale, identity, identity,
                 jax.nn.sigmoid, jax.nn.sigmoid)
    for j, epilogue in enumerate(epilogues):
        cols = slice(j * d, (j + 1) * d)
        for r in range(0, h.shape[0], PROJ_ROW_CHUNK):
            rows = slice(r, r + PROJ_ROW_CHUNK)
            o_ref[rows, cols] = epilogue(_dot(h[rows, :], w_ref[:, cols])).astype(BF16)


def _proj(x2d, mod3, norm_g, w_in_bf, ln_g, ln_b, *, seq, tm):
    m, d = x2d.shape
    n = w_in_bf.shape[1]
    tiles_per_batch = seq // tm
    kern = functools.partial(_proj_kernel, q_scale=MOBA_HEAD_DIM ** -0.5 * math.log2(math.e))
    const2 = lambda i: (0, 0)
    return pl.pallas_call(
        kern,
        grid=(m // tm,),
        in_specs=[
            pl.BlockSpec((tm, d), lambda i: (i, 0)),
            pl.BlockSpec((None, N_MOD, d), lambda i: (i // tiles_per_batch, 0, 0)),
            pl.BlockSpec((1, d), const2),
            pl.BlockSpec((d, n), const2, pipeline_mode=pl.Buffered(1)),
            pl.BlockSpec((1, d), const2),
            pl.BlockSpec((1, d), const2),
        ],
        out_specs=pl.BlockSpec((tm, n), lambda i: (i, 0)),
        out_shape=jax.ShapeDtypeStruct((m, n), BF16),
        compiler_params=pltpu.CompilerParams(
            dimension_semantics=("arbitrary",), vmem_limit_bytes=VMEM_LIMIT),
        name="proj",
    )(x2d, mod3, norm_g, w_in_bf, ln_g, ln_b)


ONES_ROWS = 16
PIPE_SLOTS = 2


def _attn_scratch_shapes(heads, n_blocks, tq):
    blk, hd = MOBA_BLOCK, MOBA_HEAD_DIM
    per_head = [
        pltpu.VMEM((n_blocks, tq), F32),
        pltpu.VMEM((1, tq), F32),
        pltpu.VMEM((hd + ONES_ROWS, tq), F32),
    ]
    per_slot_head = [
        pltpu.VMEM((blk, tq), F32),
        pltpu.VMEM((1, tq), F32),
        pltpu.VMEM((blk, tq), BF16),
        pltpu.VMEM((1, tq), F32),
    ]
    shapes = [s for s in per_head for _ in range(heads)]
    shapes += [s for s in per_slot_head for _ in range(PIPE_SLOTS * heads)]
    return shapes


def _attn_scratch_refs(scratch, heads):
    it = iter(scratch)
    per_head = [[next(it) for _ in range(heads)] for _ in range(3)]
    per_slot_head = [[[next(it) for _ in range(heads)] for _ in range(PIPE_SLOTS)] for _ in range(4)]
    return per_head + per_slot_head


def _attn_kernel(q_ref, k_ref, v_ref, o_ref, kbar_scr, vt_scr, *scratch, n_blocks, heads, tq):
    blk = MOBA_BLOCK
    hd = MOBA_HEAD_DIM
    qpb = tq // blk
    assert qpb == 2
    hcols = [slice(h * hd, (h + 1) * hd) for h in range(heads)]
    sel_scr, m_scr, acc_scr, s_scr, cmax_scr, p_scr, alpha_scr = _attn_scratch_refs(scratch, heads)

    ones_rows = (lax.broadcasted_iota(jnp.int32, (ONES_ROWS, blk), 0) == 0).astype(F32).astype(BF16)
    for b in range(n_blocks):
        rows = slice(b * blk, (b + 1) * blk)
        kbar_scr[b:b + 1, :] = jnp.sum(k_ref[rows, :].astype(F32), axis=0, keepdims=True) * (1.0 / blk)
        for h in range(heads):
            vt_scr[b, h, :hd, :] = v_ref[rows, hcols[h]].astype(F32).T.astype(BF16)
            vt_scr[b, h, hd:, :] = ones_rows

    blk_id = lax.broadcasted_iota(jnp.int32, (n_blocks, tq), 0).astype(F32)
    col_blk = (lax.broadcasted_iota(jnp.int32, (1, tq), 1) // blk).astype(F32)
    key_pos = lax.broadcasted_iota(jnp.int32, (blk, tq), 0)
    qry_pos = lax.broadcasted_iota(jnp.int32, (blk, tq), 1)

    def q_tile(qt, carry):
        qrows = pl.ds(pl.multiple_of(qt * tq, tq), tq)
        first_blk = qt * qpb
        own_blk = lax.convert_element_type(first_blk, F32) + col_blk

        for h in range(heads):
            q = q_ref[qrows, hcols[h]]
            kb_hi, kb_lo = _split_bf16(kbar_scr[:, hcols[h]])
            gate = _dot_nt(kb_hi, q) + _dot_nt(kb_lo, q)
            g = jnp.where(blk_id < own_blk, gate, NEG)
            sel_bias = jnp.where(blk_id == own_blk, 0.0, NEG)
            for _ in range(MOBA_TOPK):
                gmax = jnp.max(g, axis=0, keepdims=True)
                idx = jnp.min(jnp.where(g == gmax, blk_id, float(n_blocks)), axis=0, keepdims=True)
                pick = blk_id == idx
                sel_bias = jnp.where(pick & (idx < own_blk), 0.0, sel_bias)
                g = jnp.where(pick, -jnp.inf, g)
            sel_scr[h][...] = sel_bias
            m_scr[h][...] = jnp.full((1, tq), -jnp.inf, F32)
            acc_scr[h][...] = jnp.zeros((hd + ONES_ROWS, tq), F32)

        def raw_scores(h, kb, causal_off, slot):
            krows = pl.ds(pl.multiple_of(kb * blk, blk), blk)
            s = _dot_nt(k_ref[krows, hcols[h]], q_ref[qrows, hcols[h]]) + sel_scr[h][pl.ds(kb, 1), :]
            if causal_off is not None:
                s = jnp.where(key_pos + causal_off <= qry_pos, s, NEG)
            s_scr[slot][h][...] = s
            cmax_scr[slot][h][...] = jnp.max(s, axis=0, keepdims=True)

        def probabilities(h, slot):
            m = m_scr[h][...]
            m_new = jnp.maximum(m, cmax_scr[slot][h][...])
            alpha_scr[slot][h][...] = jnp.exp2(m - m_new)
            p_scr[slot][h][...] = jnp.exp2((s_scr[slot][h][...] - m_new).astype(BF16))
            m_scr[h][...] = m_new

        def accumulate(h, kb, slot):
            acc_scr[h][...] = (alpha_scr[slot][h][...] * acc_scr[h][...]
                               + _dot(vt_scr[kb, h], p_scr[slot][h][...]))

        def stage(score=None, prob=None, acc=None):
            if score is not None:
                for h in range(heads):
                    raw_scores(h, *score)
            if prob is not None:
                for h in range(heads):
                    probabilities(h, prob)
            if acc is not None:
                for h in range(heads):
                    accumulate(h, *acc)

        diag0, diag1 = first_blk, first_blk + 1
        stage(score=(diag0, 0, 0))
        stage(score=(diag1, blk, 1), prob=0)

        def past_pair(i, c):
            kb = 2 * i
            stage(score=(kb, None, 0), prob=1, acc=(jnp.where(i == 0, diag0, kb - 2), 0))
            stage(score=(kb + 1, None, 1), prob=0, acc=(jnp.where(i == 0, diag1, kb - 1), 1))
            return c

        lax.fori_loop(0, qt, past_pair, 0)
        stage(prob=1, acc=(jnp.where(qt == 0, diag0, first_blk - 2), 0))
        stage(acc=(jnp.where(qt == 0, diag1, first_blk - 1), 1))

        for h in range(heads):
            acc = acc_scr[h][...]
            o_ref[qrows, hcols[h]] = (acc[:hd] / acc[hd:hd + 1]).T.astype(BF16)
        return carry

    lax.fori_loop(0, n_blocks // qpb, q_tile, 0)


ATTN_HEADS_PER_STEP = 4
ATTN_Q_TILE = 512


def _attention(proj, *, batch, seq, d):
    hd = MOBA_HEAD_DIM
    blk = MOBA_BLOCK
    n_blocks = seq // blk
    hg = ATTN_HEADS_PER_STEP
    tq = ATTN_Q_TILE
    groups = MOBA_HEADS // hg
    q_off, k_off, v_off = 2 * groups, 3 * groups, 4 * groups
    kern = functools.partial(_attn_kernel, n_blocks=n_blocks, heads=hg, tq=tq)
    return pl.pallas_call(
        kern,
        grid=(batch, groups),
        in_specs=[
            pl.BlockSpec((seq, hg * hd), lambda b, g: (b, q_off + g)),
            pl.BlockSpec((seq, hg * hd), lambda b, g: (b, k_off + g)),
            pl.BlockSpec((seq, hg * hd), lambda b, g: (b, v_off + g)),
        ],
        out_specs=pl.BlockSpec((seq, hg * hd), lambda b, g: (b, g)),
        out_shape=jax.ShapeDtypeStruct((batch * seq, MOBA_HEADS * hd), BF16),
        scratch_shapes=[
            pltpu.VMEM((n_blocks, hg * hd), F32),
            pltpu.VMEM((n_blocks, hg, hd + ONES_ROWS, blk), BF16),
        ] + _attn_scratch_shapes(hg, n_blocks, tq),
        compiler_params=pltpu.CompilerParams(
            dimension_semantics=("arbitrary", "arbitrary"), vmem_limit_bytes=VMEM_LIMIT),
        name="moba_attn",
    )(proj, proj, proj)


def _merge_kernel(x_ref, u_ref, v_ref, ga_ref, gb_ref, yb_ref, ws_ref, bs_ref,
                  wpa_ref, wpb_ref, wout_ref, mod_ref, o_ref, ya_scr, *, tm):
    t = GMLP_CHUNK
    r = lax.broadcasted_iota(jnp.int32, (t, t), 0)
    c = lax.broadcasted_iota(jnp.int32, (t, t), 1)
    tril = c <= r
    for g in range(GMLP_GROUPS):
        w = jnp.where(tril, ws_ref[g], 0.0).astype(BF16)
        bcol = bs_ref[:, g:g + 1]
        cols = slice(g * t, (g + 1) * t)
        for ch in range(tm // t):
            rows = slice(ch * t, (ch + 1) * t)
            mixed = _dot(w, v_ref[rows, cols]) + bcol
            ya_scr[rows, cols] = (u_ref[rows, cols].astype(F32) * mixed).astype(BF16)

    pa = _dot(ya_scr[...], wpa_ref[...])
    pb = _dot(yb_ref[...], wpb_ref[...])
    merged = ga_ref[...].astype(F32) * pa + gb_ref[...].astype(F32) * pb
    out = _dot(merged.astype(BF16), wout_ref[...])
    o_ref[...] = x_ref[...] + mod_ref[G_M:G_M + 1, :] * out


def _merge(x2d, proj, yb, w_sp, b_sp_t, wpa, wpb, wout, mod3, *, seq, tm):
    m, d = x2d.shape
    tiles_per_batch = seq // tm
    t = GMLP_CHUNK
    kern = functools.partial(_merge_kernel, tm=tm)
    const2 = lambda i: (0, 0)
    return pl.pallas_call(
        kern,
        grid=(m // tm,),
        in_specs=[
            pl.BlockSpec((tm, d), lambda i: (i, 0)),
            pl.BlockSpec((tm, d), lambda i: (i, 0)),
            pl.BlockSpec((tm, d), lambda i: (i, 1)),
            pl.BlockSpec((tm, d), lambda i: (i, 5)),
            pl.BlockSpec((tm, d), lambda i: (i, 6)),
            pl.BlockSpec((tm, d), lambda i: (i, 0)),
            pl.BlockSpec((GMLP_GROUPS, t, t), lambda i: (0, 0, 0)),
            pl.BlockSpec((t, GMLP_GROUPS), const2),
            pl.BlockSpec((d, d), const2),
            pl.BlockSpec((d, d), const2),
            pl.BlockSpec((d, d), const2),
            pl.BlockSpec((None, N_MOD, d), lambda i: (i // tiles_per_batch, 0, 0)),
        ],
        out_specs=pl.BlockSpec((tm, d), lambda i: (i, 0)),
        out_shape=jax.ShapeDtypeStruct((m, d), F32),
        scratch_shapes=[pltpu.VMEM((tm, d), BF16)],
        compiler_params=pltpu.CompilerParams(
            dimension_semantics=("arbitrary",), vmem_limit_bytes=VMEM_LIMIT),
        name="merge",
    )(x2d, proj, proj, proj, proj, yb, w_sp, b_sp_t, wpa, wpb, wout, mod3)


def _ffn_kernel(x_ref, mod_ref, g_ref, wg_ref, wu_ref, wd_ref, gfin_ref, o_ref, *, final_norm):
    x = x_ref[...]
    gain = g_ref[...] * (1.0 + mod_ref[SC_F:SC_F + 1, :])
    h = _modulated_rmsnorm(x, gain, mod_ref[SH_F:SH_F + 1, :]).astype(BF16)
    gate = _dot(h, wg_ref[...])
    up = _dot(h, wu_ref[...])
    a = (gate * jax.nn.sigmoid(gate) * up).astype(BF16)
    y = x + mod_ref[G_F:G_F + 1, :] * _dot(a, wd_ref[...])
    if final_norm:
        y = y * lax.rsqrt(jnp.mean(y * y, axis=-1, keepdims=True) + EPS) * gfin_ref[...]
    o_ref[...] = y


def _ffn(x2d, mod3, norm_g, wg, wu, wd, g_final, *, seq, tm, final_norm):
    m, d = x2d.shape
    fh = wg.shape[1]
    tiles_per_batch = seq // tm
    kern = functools.partial(_ffn_kernel, final_norm=final_norm)
    const2 = lambda i: (0, 0)
    resident = dict(pipeline_mode=pl.Buffered(1))
    return pl.pallas_call(
        kern,
        grid=(m // tm,),
        in_specs=[
            pl.BlockSpec((tm, d), lambda i: (i, 0)),
            pl.BlockSpec((None, N_MOD, d), lambda i: (i // tiles_per_batch, 0, 0)),
            pl.BlockSpec((1, d), const2),
            pl.BlockSpec((d, fh), const2, **resident),
            pl.BlockSpec((d, fh), const2, **resident),
            pl.BlockSpec((fh, d), const2, **resident),
            pl.BlockSpec((1, d), const2),
        ],
        out_specs=pl.BlockSpec((tm, d), lambda i: (i, 0)),
        out_shape=jax.ShapeDtypeStruct((m, d), F32),
        compiler_params=pltpu.CompilerParams(
            dimension_semantics=("arbitrary",), vmem_limit_bytes=VMEM_LIMIT),
        name="ffn",
    )(x2d, mod3, norm_g, wg, wu, wd, g_final)


def kernel(x, c, w_ada, b_ada, norm_mix_g, w_in, ln_v_g, ln_v_b, w_spatial, b_spatial,
           w_proj_a, w_proj_b, w_out, norm_ffn_g, w_ffn_gate, w_ffn_up, w_ffn_down,
           norm_final_g):
    batch, seq, d = x.shape
    depth = w_ada.shape[0]
    assert d == MOBA_HEADS * MOBA_HEAD_DIM == GMLP_GROUPS * LANES
    assert seq % ATTN_Q_TILE == 0 and ATTN_Q_TILE % MOBA_BLOCK == 0
    assert w_in.shape[2] == N_SECTIONS * d

    x2d = x.reshape(batch * seq, d)
    c8 = jnp.pad(c, ((0, 8 - batch), (0, 0)))
    row = lambda a: a.reshape(1, -1)
    g_final = row(norm_final_g)

    for l in range(depth):
        mod = _adaln(c8, w_ada[l], row(b_ada[l]))
        mod3 = mod[:batch].reshape(batch, N_MOD, d)
        proj = _proj(x2d, mod3, row(norm_mix_g[l]), w_in[l].astype(BF16),
                     row(ln_v_g[l]), row(ln_v_b[l]), seq=seq, tm=512)
        yb = _attention(proj, batch=batch, seq=seq, d=d)
        x2d = _merge(x2d, proj, yb, w_spatial[l], b_spatial[l].T,
                     w_proj_a[l].astype(BF16), w_proj_b[l].astype(BF16), w_out[l].astype(BF16),
                     mod3, seq=seq, tm=512)
        x2d = _ffn(x2d, mod3, row(norm_ffn_g[l]), w_ffn_gate[l].astype(BF16),
                   w_ffn_up[l].astype(BF16), w_ffn_down[l].astype(BF16), g_final,
                   seq=seq, tm=512, final_norm=(l == depth - 1))
    return x2d.reshape(batch, seq, d)
```

```python
import functools
import math

import jax
import jax.numpy as jnp
from jax import lax
from jax.experimental import pallas as pl
from jax.experimental.pallas import tpu as pltpu

F32 = jnp.float32
BF16 = jnp.bfloat16

GMLP_CHUNK = 128
GMLP_GROUPS = 8
MOBA_HEADS = 8
MOBA_HEAD_DIM = 128
MOBA_BLOCK = 256
MOBA_TOPK = 3
N_MOD = 6
N_SECTIONS = 7
EPS = 1e-6
NEG = -1e30
LANES = 128
VMEM_LIMIT = 56 * 1024 * 1024

SH_M, SC_M, G_M, SH_F, SC_F, G_F = range(N_MOD)


def _dot(a, b):
    return jnp.dot(a, b, preferred_element_type=F32)


def _dot_nt(a, b):
    return lax.dot_general(a, b, (((1,), (1,)), ((), ())), preferred_element_type=F32)


def _split_bf16(a):
    hi = a.astype(BF16)
    lo = (a - hi.astype(F32)).astype(BF16)
    return hi, lo


def _adaln_kernel(c_ref, w_ref, b_ref, o_ref):
    c = c_ref[...]
    c_act = c * jax.nn.sigmoid(c)
    c_hi, c_lo = _split_bf16(c_act)
    w_hi, w_lo = _split_bf16(w_ref[...])
    o_ref[...] = _dot(c_hi, w_hi) + (_dot(c_hi, w_lo) + _dot(c_lo, w_hi)) + b_ref[...]


def _adaln(c8, w_ada, b_ada):
    rows, d = c8.shape
    n = w_ada.shape[1]
    tn = d
    return pl.pallas_call(
        _adaln_kernel,
        grid=(n // tn,),
        in_specs=[
            pl.BlockSpec((rows, d), lambda j: (0, 0)),
            pl.BlockSpec((d, tn), lambda j: (0, j)),
            pl.BlockSpec((1, tn), lambda j: (0, j)),
        ],
        out_specs=pl.BlockSpec((rows, tn), lambda j: (0, j)),
        out_shape=jax.ShapeDtypeStruct((rows, n), F32),
        compiler_params=pltpu.CompilerParams(
            dimension_semantics=("arbitrary",), vmem_limit_bytes=VMEM_LIMIT),
        name="adaln",
    )(c8, w_ada, b_ada)


PROJ_ROW_CHUNK = 256


def _modulated_rmsnorm(x, gain_row, shift_row):
    r = lax.rsqrt(jnp.mean(x * x, axis=-1, keepdims=True) + EPS)
    return x * r * gain_row + shift_row


def _proj_kernel(x_ref, mod_ref, g_ref, w_ref, lng_ref, lnb_ref, o_ref, *, q_scale):
    d = x_ref.shape[1]
    gain = g_ref[...] * (1.0 + mod_ref[SC_M:SC_M + 1, :])
    h = _modulated_rmsnorm(x_ref[...], gain, mod_ref[SH_M:SH_M + 1, :]).astype(BF16)

    def gelu_layernorm(acc):
        v = jax.nn.gelu(acc)
        mu = jnp.mean(v, axis=-1, keepdims=True)
        vc = v - mu
        var = jnp.mean(vc * vc, axis=-1, keepdims=True)
        return vc * lax.rsqrt(var + EPS) * lng_ref[...] + lnb_ref[...]

    identity = lambda acc: acc
    epilogues = (jax.nn.gelu, gelu_layernorm, lambda acc: acc * q_scale, identity, identity,
                 jax.nn.sigmoid, jax.nn.sigmoid)
    for j, epilogue in enumerate(epilogues):
        cols = slice(j * d, (j + 1) * d)
        for r in range(0, h.shape[0], PROJ_ROW_CHUNK):
            rows = slice(r, r + PROJ_ROW_CHUNK)
            o_ref[rows, cols] = epilogue(_dot(h[rows, :], w_ref[:, cols])).astype(BF16)


def _proj(x2d, mod3, norm_g, w_in_bf, ln_g, ln_b, *, seq, tm):
    m, d = x2d.shape
    n = w_in_bf.shape[1]
    tiles_per_batch = seq // tm
    kern = functools.partial(_proj_kernel, q_scale=MOBA_HEAD_DIM ** -0.5 * math.log2(math.e))
    const2 = lambda i: (0, 0)
    return pl.pallas_call(
        kern,
        grid=(m // tm,),
        in_specs=[
            pl.BlockSpec((tm, d), lambda i: (i, 0)),
            pl.BlockSpec((None, N_MOD, d), lambda i: (i // tiles_per_batch, 0, 0)),
            pl.BlockSpec((1, d), const2),
            pl.BlockSpec((d, n), const2, pipeline_mode=pl.Buffered(1)),
            pl.BlockSpec((1, d), const2),
            pl.BlockSpec((1, d), const2),
        ],
        out_specs=pl.BlockSpec((tm, n), lambda i: (i, 0)),
        out_shape=jax.ShapeDtypeStruct((m, n), BF16),
        compiler_params=pltpu.CompilerParams(
            dimension_semantics=("arbitrary",), vmem_limit_bytes=VMEM_LIMIT),
        name="proj",
    )(x2d, mod3, norm_g, w_in_bf, ln_g, ln_b)


ONES_ROWS = 16
PIPE_SLOTS = 2


def _attn_scratch_shapes(heads, n_blocks, tq):
    blk, hd = MOBA_BLOCK, MOBA_HEAD_DIM
    per_head = [
        pltpu.VMEM((n_blocks, tq), F32),
        pltpu.VMEM((1, tq), F32),
        pltpu.VMEM((hd + ONES_ROWS, tq), F32),
    ]
    per_slot_head = [
        pltpu.VMEM((blk, tq), F32),
        pltpu.VMEM((1, tq), F32),
        pltpu.VMEM((blk, tq), BF16),
        pltpu.VMEM((1, tq), F32),
    ]
    shapes = [s for s in per_head for _ in range(heads)]
    shapes += [s for s in per_slot_head for _ in range(PIPE_SLOTS * heads)]
    return shapes


def _attn_scratch_refs(scratch, heads):
    it = iter(scratch)
    per_head = [[next(it) for _ in range(heads)] for _ in range(3)]
    per_slot_head = [[[next(it) for _ in range(heads)] for _ in range(PIPE_SLOTS)] for _ in range(4)]
    return per_head + per_slot_head


def _attn_kernel(q_ref, k_ref, v_ref, o_ref, kbar_scr, vt_scr, *scratch, n_blocks, heads, tq):
    blk = MOBA_BLOCK
    hd = MOBA_HEAD_DIM
    qpb = tq // blk
    assert qpb == 2
    hcols = [slice(h * hd, (h + 1) * hd) for h in range(heads)]
    sel_scr, m_scr, acc_scr, s_scr, cmax_scr, p_scr, alpha_scr = _attn_scratch_refs(scratch, heads)

    ones_rows = (lax.broadcasted_iota(jnp.int32, (ONES_ROWS, blk), 0) == 0).astype(F32).astype(BF16)
    for b in range(n_blocks):
        rows = slice(b * blk, (b + 1) * blk)
        kbar_scr[b:b + 1, :] = jnp.sum(k_ref[rows, :].astype(F32), axis=0, keepdims=True) * (1.0 / blk)
        for h in range(heads):
            vt_scr[b, h, :hd, :] = v_ref[rows, hcols[h]].astype(F32).T.astype(BF16)
            vt_scr[b, h, hd:, :] = ones_rows

    blk_id = lax.broadcasted_iota(jnp.int32, (n_blocks, tq), 0).astype(F32)
    col_blk = (lax.broadcasted_iota(jnp.int32, (1, tq), 1) // blk).astype(F32)
    key_pos = lax.broadcasted_iota(jnp.int32, (blk, tq), 0)
    qry_pos = lax.broadcasted_iota(jnp.int32, (blk, tq), 1)

    def q_tile(qt, carry):
        qrows = pl.ds(pl.multiple_of(qt * tq, tq), tq)
        first_blk = qt * qpb
        own_blk = lax.convert_element_type(first_blk, F32) + col_blk

        for h in range(heads):
            q = q_ref[qrows, hcols[h]]
            kb_hi, kb_lo = _split_bf16(kbar_scr[:, hcols[h]])
            gate = _dot_nt(kb_hi, q) + _dot_nt(kb_lo, q)
            g = jnp.where(blk_id < own_blk, gate, NEG)
            sel_bias = jnp.where(blk_id == own_blk, 0.0, NEG)
            for _ in range(MOBA_TOPK):
                gmax = jnp.max(g, axis=0, keepdims=True)
                idx = jnp.min(jnp.where(g == gmax, blk_id, float(n_blocks)), axis=0, keepdims=True)
                pick = blk_id == idx
                sel_bias = jnp.where(pick & (idx < own_blk), 0.0, sel_bias)
                g = jnp.where(pick, -jnp.inf, g)
            sel_scr[h][...] = sel_bias
            m_scr[h][...] = jnp.full((1, tq), -jnp.inf, F32)
            acc_scr[h][...] = jnp.zeros((hd + ONES_ROWS, tq), F32)

        def raw_scores(h, kb, causal_off, slot):
            krows = pl.ds(pl.multiple_of(kb * blk, blk), blk)
            s = _dot_nt(k_ref[krows, hcols[h]], q_ref[qrows, hcols[h]]) + sel_scr[h][pl.ds(kb, 1), :]
            if causal_off is not None:
                s = jnp.where(key_pos + causal_off <= qry_pos, s, NEG)
            s_scr[slot][h][...] = s
            cmax_scr[slot][h][...] = jnp.max(s, axis=0, keepdims=True)

        def probabilities(h, slot):
            m = m_scr[h][...]
            m_new = jnp.maximum(m, cmax_scr[slot][h][...])
            alpha_scr[slot][h][...] = jnp.exp2(m - m_new)
            p_scr[slot][h][...] = jnp.exp2(s_scr[slot][h][...] - m_new).astype(BF16)
            m_scr[h][...] = m_new

        def accumulate(h, kb, slot):
            acc_scr[h][...] = (alpha_scr[slot][h][...] * acc_scr[h][...]
                               + _dot(vt_scr[kb, h], p_scr[slot][h][...]))

        def stage(score=None, prob=None, acc=None):
            if score is not None:
                for h in range(heads):
                    raw_scores(h, *score)
            if prob is not None:
                for h in range(heads):
                    probabilities(h, prob)
            if acc is not None:
                for h in range(heads):
                    accumulate(h, *acc)

        diag0, diag1 = first_blk, first_blk + 1
        stage(score=(diag0, 0, 0))
        stage(score=(diag1, blk, 1), prob=0)

        def past_pair(i, c):
            kb = 2 * i
            stage(score=(kb, None, 0), prob=1, acc=(jnp.where(i == 0, diag0, kb - 2), 0))
            stage(score=(kb + 1, None, 1), prob=0, acc=(jnp.where(i == 0, diag1, kb - 1), 1))
            return c

        lax.fori_loop(0, qt, past_pair, 0)
        stage(prob=1, acc=(jnp.where(qt == 0, diag0, first_blk - 2), 0))
        stage(acc=(jnp.where(qt == 0, diag1, first_blk - 1), 1))

        for h in range(heads):
            acc = acc_scr[h][...]
            o_ref[qrows, hcols[h]] = (acc[:hd] / acc[hd:hd + 1]).T.astype(BF16)
        return carry

    lax.fori_loop(0, n_blocks // qpb, q_tile, 0)


ATTN_HEADS_PER_STEP = 4
ATTN_Q_TILE = 512


def _attention(proj, *, batch, seq, d):
    hd = MOBA_HEAD_DIM
    blk = MOBA_BLOCK
    n_blocks = seq // blk
    hg = ATTN_HEADS_PER_STEP
    tq = ATTN_Q_TILE
    groups = MOBA_HEADS // hg
    q_off, k_off, v_off = 2 * groups, 3 * groups, 4 * groups
    kern = functools.partial(_attn_kernel, n_blocks=n_blocks, heads=hg, tq=tq)
    return pl.pallas_call(
        kern,
        grid=(batch, groups),
        in_specs=[
            pl.BlockSpec((seq, hg * hd), lambda b, g: (b, q_off + g)),
            pl.BlockSpec((seq, hg * hd), lambda b, g: (b, k_off + g)),
            pl.BlockSpec((seq, hg * hd), lambda b, g: (b, v_off + g)),
        ],
        out_specs=pl.BlockSpec((seq, hg * hd), lambda b, g: (b, g)),
        out_shape=jax.ShapeDtypeStruct((batch * seq, MOBA_HEADS * hd), BF16),
        scratch_shapes=[
            pltpu.VMEM((n_blocks, hg * hd), F32),
            pltpu.VMEM((n_blocks, hg, hd + ONES_ROWS, blk), BF16),
        ] + _attn_scratch_shapes(hg, n_blocks, tq),
        compiler_params=pltpu.CompilerParams(
            dimension_semantics=("arbitrary", "arbitrary"), vmem_limit_bytes=VMEM_LIMIT),
        name="moba_attn",
    )(proj, proj, proj)


def _merge_kernel(x_ref, u_ref, v_ref, ga_ref, gb_ref, yb_ref, ws_ref, bs_ref,
                  wpa_ref, wpb_ref, wout_ref, mod_ref, o_ref, ya_scr, *, tm):
    t = GMLP_CHUNK
    r = lax.broadcasted_iota(jnp.int32, (t, t), 0)
    c = lax.broadcasted_iota(jnp.int32, (t, t), 1)
    tril = c <= r
    for g in range(GMLP_GROUPS):
        w = jnp.where(tril, ws_ref[g], 0.0).astype(BF16)
        bcol = bs_ref[:, g:g + 1]
        cols = slice(g * t, (g + 1) * t)
        for ch in range(tm // t):
            rows = slice(ch * t, (ch + 1) * t)
            mixed = _dot(w, v_ref[rows, cols]) + bcol
            ya_scr[rows, cols] = (u_ref[rows, cols].astype(F32) * mixed).astype(BF16)

    pa = _dot(ya_scr[...], wpa_ref[...])
    pb = _dot(yb_ref[...], wpb_ref[...])
    merged = ga_ref[...].astype(F32) * pa + gb_ref[...].astype(F32) * pb
    out = _dot(merged.astype(BF16), wout_ref[...])
    o_ref[...] = x_ref[...] + mod_ref[G_M:G_M + 1, :] * out


def _merge(x2d, proj, yb, w_sp, b_sp_t, wpa, wpb, wout, mod3, *, seq, tm):
    m, d = x2d.shape
    tiles_per_batch = seq // tm
    t = GMLP_CHUNK
    kern = functools.partial(_merge_kernel, tm=tm)
    const2 = lambda i: (0, 0)
    return pl.pallas_call(
        kern,
        grid=(m // tm,),
        in_specs=[
            pl.BlockSpec((tm, d), lambda i: (i, 0)),
            pl.BlockSpec((tm, d), lambda i: (i, 0)),
            pl.BlockSpec((tm, d), lambda i: (i, 1)),
            pl.BlockSpec((tm, d), lambda i: (i, 5)),
            pl.BlockSpec((tm, d), lambda i: (i, 6)),
            pl.BlockSpec((tm, d), lambda i: (i, 0)),
            pl.BlockSpec((GMLP_GROUPS, t, t), lambda i: (0, 0, 0)),
            pl.BlockSpec((t, GMLP_GROUPS), const2),
            pl.BlockSpec((d, d), const2),
            pl.BlockSpec((d, d), const2),
            pl.BlockSpec((d, d), const2),
            pl.BlockSpec((None, N_MOD, d), lambda i: (i // tiles_per_batch, 0, 0)),
        ],
        out_specs=pl.BlockSpec((tm, d), lambda i: (i, 0)),
        out_shape=jax.ShapeDtypeStruct((m, d), F32),
        scratch_shapes=[pltpu.VMEM((tm, d), BF16)],
        compiler_params=pltpu.CompilerParams(
            dimension_semantics=("arbitrary",), vmem_limit_bytes=VMEM_LIMIT),
        name="merge",
    )(x2d, proj, proj, proj, proj, yb, w_sp, b_sp_t, wpa, wpb, wout, mod3)


def _ffn_kernel(x_ref, mod_ref, g_ref, wg_ref, wu_ref, wd_ref, gfin_ref, o_ref, *, final_norm):
    x = x_ref[...]
    gain = g_ref[...] * (1.0 + mod_ref[SC_F:SC_F + 1, :])
    h = _modulated_rmsnorm(x, gain, mod_ref[SH_F:SH_F + 1, :]).astype(BF16)
    gate = _dot(h, wg_ref[...])
    up = _dot(h, wu_ref[...])
    a = (gate * jax.nn.sigmoid(gate) * up).astype(BF16)
    y = x + mod_ref[G_F:G_F + 1, :] * _dot(a, wd_ref[...])
    if final_norm:
        y = y * lax.rsqrt(jnp.mean(y * y, axis=-1, keepdims=True) + EPS) * gfin_ref[...]
    o_ref[...] = y


def _ffn(x2d, mod3, norm_g, wg, wu, wd, g_final, *, seq, tm, final_norm):
    m, d = x2d.shape
    fh = wg.shape[1]
    tiles_per_batch = seq // tm
    kern = functools.partial(_ffn_kernel, final_norm=final_norm)
    const2 = lambda i: (0, 0)
    resident = dict(pipeline_mode=pl.Buffered(1))
    return pl.pallas_call(
        kern,
        grid=(m // tm,),
        in_specs=[
            pl.BlockSpec((tm, d), lambda i: (i, 0)),
            pl.BlockSpec((None, N_MOD, d), lambda i: (i // tiles_per_batch, 0, 0)),
            pl.BlockSpec((1, d), const2),
            pl.BlockSpec((d, fh), const2, **resident),
            pl.BlockSpec((d, fh), const2, **resident),
            pl.BlockSpec((fh, d), const2, **resident),
            pl.BlockSpec((1, d), const2),
        ],
        out_specs=pl.BlockSpec((tm, d), lambda i: (i, 0)),
        out_shape=jax.ShapeDtypeStruct((m, d), F32),
        compiler_params=pltpu.CompilerParams(
            dimension_semantics=("arbitrary",), vmem_limit_bytes=VMEM_LIMIT),
        name="ffn",
    )(x2d, mod3, norm_g, wg, wu, wd, g_final)


def kernel(x, c, w_ada, b_ada, norm_mix_g, w_in, ln_v_g, ln_v_b, w_spatial, b_spatial,
           w_proj_a, w_proj_b, w_out, norm_ffn_g, w_ffn_gate, w_ffn_up, w_ffn_down,
           norm_final_g):
    batch, seq, d = x.shape
    depth = w_ada.shape[0]
    assert d == MOBA_HEADS * MOBA_HEAD_DIM == GMLP_GROUPS * LANES
    assert seq % ATTN_Q_TILE == 0 and ATTN_Q_TILE % MOBA_BLOCK == 0
    assert w_in.shape[2] == N_SECTIONS * d

    x2d = x.reshape(batch * seq, d)
    c8 = jnp.pad(c, ((0, 8 - batch), (0, 0)))
    row = lambda a: a.reshape(1, -1)
    g_final = row(norm_final_g)

    for l in range(depth):
        mod = _adaln(c8, w_ada[l], row(b_ada[l]))
        mod3 = mod[:batch].reshape(batch, N_MOD, d)
        proj = _proj(x2d, mod3, row(norm_mix_g[l]), w_in[l].astype(BF16),
                     row(ln_v_g[l]), row(ln_v_b[l]), seq=seq, tm=512)
        yb = _attention(proj, batch=batch, seq=seq, d=d)
        x2d = _merge(x2d, proj, yb, w_spatial[l], b_spatial[l].T,
                     w_proj_a[l].astype(BF16), w_proj_b[l].astype(BF16), w_out[l].astype(BF16),
                     mod3, seq=seq, tm=512)
        x2d = _ffn(x2d, mod3, row(norm_ffn_g[l]), w_ffn_gate[l].astype(BF16),
                   w_ffn_up[l].astype(BF16), w_ffn_down[l].astype(BF16), g_final,
                   seq=seq, tm=512, final_norm=(l == depth - 1))
    return x2d.reshape(batch, seq, d)
```

```python
import functools
import math

import jax
import jax.numpy as jnp
from jax import lax
from jax.experimental import pallas as pl
from jax.experimental.pallas import tpu as pltpu

F32 = jnp.float32
BF16 = jnp.bfloat16

GMLP_CHUNK = 128
GMLP_GROUPS = 8
MOBA_HEADS = 8
MOBA_HEAD_DIM = 128
MOBA_BLOCK = 256
MOBA_TOPK = 3
N_MOD = 6
N_SECTIONS = 7
EPS = 1e-6
NEG = -1e30
LANES = 128
VMEM_LIMIT = 56 * 1024 * 1024

SH_M, SC_M, G_M, SH_F, SC_F, G_F = range(N_MOD)


def _dot(a, b):
    return jnp.dot(a, b, preferred_element_type=F32)


def _dot_nt(a, b):
    return lax.dot_general(a, b, (((1,), (1,)), ((), ())), preferred_element_type=F32)


def _split_bf16(a):
    hi = a.astype(BF16)
    lo = (a - hi.astype(F32)).astype(BF16)
    return hi, lo


def _adaln_kernel(c_ref, w_ref, b_ref, o_ref):
    c = c_ref[...]
    rows = c.shape[0]
    c_act = c * jax.nn.sigmoid(c)
    c_hi, c_lo = _split_bf16(c_act)
    y = _dot(jnp.concatenate([c_hi, c_lo], axis=0), w_ref[...].astype(BF16))
    o_ref[...] = y[:rows] + y[rows:] + b_ref[...]


def _adaln(c8, w_ada, b_ada):
    rows, d = c8.shape
    n = w_ada.shape[1]
    tn = d
    return pl.pallas_call(
        _adaln_kernel,
        grid=(n // tn,),
        in_specs=[
            pl.BlockSpec((rows, d), lambda j: (0, 0)),
            pl.BlockSpec((d, tn), lambda j: (0, j)),
            pl.BlockSpec((1, tn), lambda j: (0, j)),
        ],
        out_specs=pl.BlockSpec((rows, tn), lambda j: (0, j)),
        out_shape=jax.ShapeDtypeStruct((rows, n), F32),
        compiler_params=pltpu.CompilerParams(
            dimension_semantics=("arbitrary",), vmem_limit_bytes=VMEM_LIMIT),
        name="adaln",
    )(c8, w_ada, b_ada)


PROJ_ROW_CHUNK = 256


def _modulated_rmsnorm(x, gain_row, shift_row):
    r = lax.rsqrt(jnp.mean(x * x, axis=-1, keepdims=True) + EPS)
    return x * r * gain_row + shift_row


def _proj_kernel(x_ref, mod_ref, g_ref, w_ref, lng_ref, lnb_ref, o_ref, *, q_scale):
    d = x_ref.shape[1]
    gain = g_ref[...] * (1.0 + mod_ref[SC_M:SC_M + 1, :])
    h = _modulated_rmsnorm(x_ref[...], gain, mod_ref[SH_M:SH_M + 1, :]).astype(BF16)

    def gelu_layernorm(acc):
        v = jax.nn.gelu(acc)
        mu = jnp.mean(v, axis=-1, keepdims=True)
        vc = v - mu
        var = jnp.mean(vc * vc, axis=-1, keepdims=True)
        return vc * lax.rsqrt(var + EPS) * lng_ref[...] + lnb_ref[...]

    identity = lambda acc: acc
    epilogues = (jax.nn.gelu, gelu_layernorm, lambda acc: acc * q_scale, identity, identity,
                 jax.nn.sigmoid, jax.nn.sigmoid)
    for j, epilogue in enumerate(epilogues):
        cols = slice(j * d, (j + 1) * d)
        for r in range(0, h.shape[0], PROJ_ROW_CHUNK):
            rows = slice(r, r + PROJ_ROW_CHUNK)
            o_ref[rows, cols] = epilogue(_dot(h[rows, :], w_ref[:, cols])).astype(BF16)


def _proj(x2d, mod3, norm_g, w_in_bf, ln_g, ln_b, *, seq, tm):
    m, d = x2d.shape
    n = w_in_bf.shape[1]
    tiles_per_batch = seq // tm
    kern = functools.partial(_proj_kernel, q_scale=MOBA_HEAD_DIM ** -0.5 * math.log2(math.e))
    const2 = lambda i: (0, 0)
    return pl.pallas_call(
        kern,
        grid=(m // tm,),
        in_specs=[
            pl.BlockSpec((tm, d), lambda i: (i, 0)),
            pl.BlockSpec((None, N_MOD, d), lambda i: (i // tiles_per_batch, 0, 0)),
            pl.BlockSpec((1, d), const2),
            pl.BlockSpec((d, n), const2, pipeline_mode=pl.Buffered(1)),
            pl.BlockSpec((1, d), const2),
            pl.BlockSpec((1, d), const2),
        ],
        out_specs=pl.BlockSpec((tm, n), lambda i: (i, 0)),
        out_shape=jax.ShapeDtypeStruct((m, n), BF16),
        compiler_params=pltpu.CompilerParams(
            dimension_semantics=("arbitrary",), vmem_limit_bytes=VMEM_LIMIT),
        name="proj",
    )(x2d, mod3, norm_g, w_in_bf, ln_g, ln_b)


ONES_ROWS = 16


def _attn_kernel(q_ref, k_ref, v_ref, o_ref, kbar_scr, vt_scr, sel_scr, s_scr, p_scr,
                 m_scr, alpha_scr, acc_scr, *, n_blocks, heads, tq):
    blk = MOBA_BLOCK
    hd = MOBA_HEAD_DIM
    qpb = tq // blk
    assert qpb == 2
    hcols = [slice(h * hd, (h + 1) * hd) for h in range(heads)]

    ones_rows = (lax.broadcasted_iota(jnp.int32, (ONES_ROWS, blk), 0) == 0).astype(F32).astype(BF16)
    for b in range(n_blocks):
        rows = slice(b * blk, (b + 1) * blk)
        kbar_scr[b:b + 1, :] = jnp.sum(k_ref[rows, :].astype(F32), axis=0, keepdims=True) * (1.0 / blk)
        for h in range(heads):
            vt_scr[b, h, :hd, :] = v_ref[rows, hcols[h]].astype(F32).T.astype(BF16)
            vt_scr[b, h, hd:, :] = ones_rows

    blk_id = lax.broadcasted_iota(jnp.int32, (n_blocks, tq), 0).astype(F32)
    col_blk = (lax.broadcasted_iota(jnp.int32, (1, tq), 1) // blk).astype(F32)
    key_pos = lax.broadcasted_iota(jnp.int32, (blk, tq), 0)
    qry_pos = lax.broadcasted_iota(jnp.int32, (blk, tq), 1)

    def q_tile(qt, carry):
        qrows = pl.ds(pl.multiple_of(qt * tq, tq), tq)
        first_blk = qt * qpb
        own_blk = lax.convert_element_type(first_blk, F32) + col_blk

        for h in range(heads):
            q = q_ref[qrows, hcols[h]]
            kb_hi, kb_lo = _split_bf16(kbar_scr[:, hcols[h]])
            gate = _dot_nt(kb_hi, q) + _dot_nt(kb_lo, q)
            g = jnp.where(blk_id < own_blk, gate, NEG)
            selected = jnp.where(blk_id == own_blk, 1.0, 0.0)
            for _ in range(MOBA_TOPK):
                gmax = jnp.max(g, axis=0, keepdims=True)
                idx = jnp.min(jnp.where(g == gmax, blk_id, float(n_blocks)), axis=0, keepdims=True)
                pick = blk_id == idx
                selected = jnp.where(pick & (idx < own_blk), 1.0, selected)
                g = jnp.where(pick, -jnp.inf, g)
            sel_scr[h] = selected
            m_scr[h] = jnp.full((1, tq), -jnp.inf, F32)
            acc_scr[h] = jnp.zeros((hd + ONES_ROWS, tq), F32)

        def scores(kb, causal_off, slot):
            krows = pl.ds(pl.multiple_of(kb * blk, blk), blk)
            cmax = []
            for h in range(heads):
                s = _dot_nt(k_ref[krows, hcols[h]], q_ref[qrows, hcols[h]])
                if causal_off is not None:
                    s = jnp.where(key_pos + causal_off <= qry_pos, s, NEG)
                s_scr[h] = s
                cmax.append(jnp.max(s, axis=0, keepdims=True))
            for h in range(heads):
                selected = sel_scr[h, pl.ds(kb, 1), :] > 0.5
                m = m_scr[h]
                m_new = jnp.where(selected, jnp.maximum(m, cmax[h]), m)
                alpha_scr[slot, h] = jnp.where(selected, jnp.exp2(m - m_new), 1.0)
                p_scr[slot, h] = jnp.exp2(s_scr[h] - jnp.where(selected, m_new, -NEG)).astype(BF16)
                m_scr[h] = m_new

        def accumulate(kb, slot):
            for h in range(heads):
                acc_scr[h] = alpha_scr[slot, h] * acc_scr[h] + _dot(vt_scr[kb, h], p_scr[slot, h])

        scores(first_blk, 0, 0)
        scores(first_blk + 1, blk, 1)
        accumulate(first_blk, 0)

        def past_pair(i, c):
            kb0 = 2 * i
            prev = jnp.where(i == 0, first_blk + 1, kb0 - 1)
            scores(kb0, None, 0)
            accumulate(prev, 1)
            scores(kb0 + 1, None, 1)
            accumulate(kb0, 0)
            return c

        lax.fori_loop(0, qt, past_pair, 0)
        accumulate(jnp.where(qt == 0, first_blk + 1, first_blk - 1), 1)

        for h in range(heads):
            acc = acc_scr[h]
            o_ref[qrows, hcols[h]] = (acc[:hd] / acc[hd:hd + 1]).T.astype(BF16)
        return carry

    lax.fori_loop(0, n_blocks // qpb, q_tile, 0)


ATTN_HEADS_PER_STEP = 4
ATTN_Q_TILE = 512


def _attention(proj, *, batch, seq, d):
    hd = MOBA_HEAD_DIM
    blk = MOBA_BLOCK
    n_blocks = seq // blk
    hg = ATTN_HEADS_PER_STEP
    tq = ATTN_Q_TILE
    groups = MOBA_HEADS // hg
    q_off, k_off, v_off = 2 * groups, 3 * groups, 4 * groups
    kern = functools.partial(_attn_kernel, n_blocks=n_blocks, heads=hg, tq=tq)
    return pl.pallas_call(
        kern,
        grid=(batch, groups),
        in_specs=[
            pl.BlockSpec((seq, hg * hd), lambda b, g: (b, q_off + g)),
            pl.BlockSpec((seq, hg * hd), lambda b, g: (b, k_off + g)),
            pl.BlockSpec((seq, hg * hd), lambda b, g: (b, v_off + g)),
        ],
        out_specs=pl.BlockSpec((seq, hg * hd), lambda b, g: (b, g)),
        out_shape=jax.ShapeDtypeStruct((batch * seq, MOBA_HEADS * hd), BF16),
        scratch_shapes=[
            pltpu.VMEM((n_blocks, hg * hd), F32),
            pltpu.VMEM((n_blocks, hg, hd + ONES_ROWS, blk), BF16),
            pltpu.VMEM((hg, n_blocks, tq), F32),
            pltpu.VMEM((hg, blk, tq), F32),
            pltpu.VMEM((2, hg, blk, tq), BF16),
            pltpu.VMEM((hg, 1, tq), F32),
            pltpu.VMEM((2, hg, 1, tq), F32),
            pltpu.VMEM((hg, hd + ONES_ROWS, tq), F32),
        ],
        compiler_params=pltpu.CompilerParams(
            dimension_semantics=("arbitrary", "arbitrary"), vmem_limit_bytes=VMEM_LIMIT),
        name="moba_attn",
    )(proj, proj, proj)


def _merge_kernel(x_ref, u_ref, v_ref, ga_ref, gb_ref, yb_ref, ws_ref, bs_ref,
                  wpa_ref, wpb_ref, wout_ref, mod_ref, o_ref, ya_scr, *, tm):
    t = GMLP_CHUNK
    r = lax.broadcasted_iota(jnp.int32, (t, t), 0)
    c = lax.broadcasted_iota(jnp.int32, (t, t), 1)
    tril = c <= r
    for g in range(GMLP_GROUPS):
        w = jnp.where(tril, ws_ref[g], 0.0).astype(BF16)
        bcol = bs_ref[:, g:g + 1]
        cols = slice(g * t, (g + 1) * t)
        for ch in range(tm // t):
            rows = slice(ch * t, (ch + 1) * t)
            mixed = _dot(w, v_ref[rows, cols]) + bcol
            ya_scr[rows, cols] = (u_ref[rows, cols].astype(F32) * mixed).astype(BF16)

    pa = _dot(ya_scr[...], wpa_ref[...])
    pb = _dot(yb_ref[...], wpb_ref[...])
    merged = ga_ref[...].astype(F32) * pa + gb_ref[...].astype(F32) * pb
    out = _dot(merged.astype(BF16), wout_ref[...])
    o_ref[...] = x_ref[...] + mod_ref[G_M:G_M + 1, :] * out


def _merge(x2d, proj, yb, w_sp, b_sp_t, wpa, wpb, wout, mod3, *, seq, tm):
    m, d = x2d.shape
    tiles_per_batch = seq // tm
    t = GMLP_CHUNK
    kern = functools.partial(_merge_kernel, tm=tm)
    const2 = lambda i: (0, 0)
    return pl.pallas_call(
        kern,
        grid=(m // tm,),
        in_specs=[
            pl.BlockSpec((tm, d), lambda i: (i, 0)),
            pl.BlockSpec((tm, d), lambda i: (i, 0)),
            pl.BlockSpec((tm, d), lambda i: (i, 1)),
            pl.BlockSpec((tm, d), lambda i: (i, 5)),
            pl.BlockSpec((tm, d), lambda i: (i, 6)),
            pl.BlockSpec((tm, d), lambda i: (i, 0)),
            pl.BlockSpec((GMLP_GROUPS, t, t), lambda i: (0, 0, 0)),
            pl.BlockSpec((t, GMLP_GROUPS), const2),
            pl.BlockSpec((d, d), const2),
            pl.BlockSpec((d, d), const2),
            pl.BlockSpec((d, d), const2),
            pl.BlockSpec((None, N_MOD, d), lambda i: (i // tiles_per_batch, 0, 0)),
        ],
        out_specs=pl.BlockSpec((tm, d), lambda i: (i, 0)),
        out_shape=jax.ShapeDtypeStruct((m, d), F32),
        scratch_shapes=[pltpu.VMEM((tm, d), BF16)],
        compiler_params=pltpu.CompilerParams(
            dimension_semantics=("arbitrary",), vmem_limit_bytes=VMEM_LIMIT),
        name="merge",
    )(x2d, proj, proj, proj, proj, yb, w_sp, b_sp_t, wpa, wpb, wout, mod3)


def _ffn_kernel(x_ref, mod_ref, g_ref, wg_ref, wu_ref, wd_ref, gfin_ref, o_ref, *, final_norm):
    x = x_ref[...]
    gain = g_ref[...] * (1.0 + mod_ref[SC_F:SC_F + 1, :])
    h = _modulated_rmsnorm(x, gain, mod_ref[SH_F:SH_F + 1, :]).astype(BF16)
    gate = _dot(h, wg_ref[...])
    up = _dot(h, wu_ref[...])
    a = (gate * jax.nn.sigmoid(gate) * up).astype(BF16)
    y = x + mod_ref[G_F:G_F + 1, :] * _dot(a, wd_ref[...])
    if final_norm:
        y = y * lax.rsqrt(jnp.mean(y * y, axis=-1, keepdims=True) + EPS) * gfin_ref[...]
    o_ref[...] = y


def _ffn(x2d, mod3, norm_g, wg, wu, wd, g_final, *, seq, tm, final_norm):
    m, d = x2d.shape
    fh = wg.shape[1]
    tiles_per_batch = seq // tm
    kern = functools.partial(_ffn_kernel, final_norm=final_norm)
    const2 = lambda i: (0, 0)
    resident = dict(pipeline_mode=pl.Buffered(1))
    return pl.pallas_call(
        kern,
        grid=(m // tm,),
        in_specs=[
            pl.BlockSpec((tm, d), lambda i: (i, 0)),
            pl.BlockSpec((None, N_MOD, d), lambda i: (i // tiles_per_batch, 0, 0)),
            pl.BlockSpec((1, d), const2),
            pl.BlockSpec((d, fh), const2, **resident),
            pl.BlockSpec((d, fh), const2, **resident),
            pl.BlockSpec((fh, d), const2, **resident),
            pl.BlockSpec((1, d), const2),
        ],
        out_specs=pl.BlockSpec((tm, d), lambda i: (i, 0)),
        out_shape=jax.ShapeDtypeStruct((m, d), F32),
        compiler_params=pltpu.CompilerParams(
            dimension_semantics=("arbitrary",), vmem_limit_bytes=VMEM_LIMIT),
        name="ffn",
    )(x2d, mod3, norm_g, wg, wu, wd, g_final)


def kernel(x, c, w_ada, b_ada, norm_mix_g, w_in, ln_v_g, ln_v_b, w_spatial, b_spatial,
           w_proj_a, w_proj_b, w_out, norm_ffn_g, w_ffn_gate, w_ffn_up, w_ffn_down,
           norm_final_g):
    batch, seq, d = x.shape
    depth = w_ada.shape[0]
    assert d == MOBA_HEADS * MOBA_HEAD_DIM == GMLP_GROUPS * LANES
    assert seq % ATTN_Q_TILE == 0 and ATTN_Q_TILE % MOBA_BLOCK == 0
    assert w_in.shape[2] == N_SECTIONS * d

    x2d = x.reshape(batch * seq, d)
    c8 = jnp.pad(c, ((0, 8 - batch), (0, 0)))
    row = lambda a: a.reshape(1, -1)
    g_final = row(norm_final_g)

    for l in range(depth):
        mod = _adaln(c8, w_ada[l], row(b_ada[l]))
        mod3 = mod[:batch].reshape(batch, N_MOD, d)
        proj = _proj(x2d, mod3, row(norm_mix_g[l]), w_in[l].astype(BF16),
                     row(ln_v_g[l]), row(ln_v_b[l]), seq=seq, tm=512)
        yb = _attention(proj, batch=batch, seq=seq, d=d)
        x2d = _merge(x2d, proj, yb, w_spatial[l], b_spatial[l].T,
                     w_proj_a[l].astype(BF16), w_proj_b[l].astype(BF16), w_out[l].astype(BF16),
                     mod3, seq=seq, tm=512)
        x2d = _ffn(x2d, mod3, row(norm_ffn_g[l]), w_ffn_gate[l].astype(BF16),
                   w_ffn_up[l].astype(BF16), w_ffn_down[l].astype(BF16), g_final,
                   seq=seq, tm=512, final_norm=(l == depth - 1))
    return x2d.reshape(batch, seq, d)
```

```python
import functools
import math

import jax
import jax.numpy as jnp
from jax import lax
from jax.experimental import pallas as pl
from jax.experimental.pallas import tpu as pltpu

F32 = jnp.float32
BF16 = jnp.bfloat16

GMLP_CHUNK = 128
GMLP_GROUPS = 8
MOBA_HEADS = 8
MOBA_HEAD_DIM = 128
MOBA_BLOCK = 256
MOBA_TOPK = 3
N_MOD = 6
N_SECTIONS = 7
EPS = 1e-6
NEG = -1e30
LANES = 128
VMEM_LIMIT = 56 * 1024 * 1024

SH_M, SC_M, G_M, SH_F, SC_F, G_F = range(N_MOD)


def _dot(a, b):
    return jnp.dot(a, b, preferred_element_type=F32)


def _dot_nt(a, b):
    return lax.dot_general(a, b, (((1,), (1,)), ((), ())), preferred_element_type=F32)


def _split_bf16(a):
    hi = a.astype(BF16)
    lo = (a - hi.astype(F32)).astype(BF16)
    return hi, lo


WEIGHT_CHUNK = 256


def _load_weight_as_bf16(w_hbm, w_bf, stage, sem, *, axis):
    chunk = stage.shape[1 + axis]
    n = w_hbm.shape[axis] // chunk
    assert n * chunk == w_hbm.shape[axis] and stage.shape[2 - axis] == w_hbm.shape[1 - axis]

    def piece(ref, c):
        idx = [slice(None), slice(None)]
        idx[axis] = pl.ds(c * chunk, chunk)
        return ref.at[tuple(idx)]

    def copy(c):
        return pltpu.make_async_copy(piece(w_hbm, c), stage.at[c % 2], sem.at[c % 2])

    copy(0).start()
    for c in range(n):
        if c + 1 < n:
            copy(c + 1).start()
        copy(c).wait()
        piece(w_bf, c)[...] = stage[c % 2].astype(BF16)


def _adaln_kernel(c_ref, w_ref, b_ref, o_ref):
    c = c_ref[...]
    rows = c.shape[0]
    c_act = c * jax.nn.sigmoid(c)
    c_hi, c_lo = _split_bf16(c_act)
    y = _dot(jnp.concatenate([c_hi, c_lo], axis=0), w_ref[...].astype(BF16))
    o_ref[...] = y[:rows] + y[rows:] + b_ref[...]


def _adaln(c8, w_ada, b_ada):
    rows, d = c8.shape
    n = w_ada.shape[1]
    tn = d
    return pl.pallas_call(
        _adaln_kernel,
        grid=(n // tn,),
        in_specs=[
            pl.BlockSpec((rows, d), lambda j: (0, 0)),
            pl.BlockSpec((d, tn), lambda j: (0, j)),
            pl.BlockSpec((1, tn), lambda j: (0, j)),
        ],
        out_specs=pl.BlockSpec((rows, tn), lambda j: (0, j)),
        out_shape=jax.ShapeDtypeStruct((rows, n), F32),
        compiler_params=pltpu.CompilerParams(
            dimension_semantics=("arbitrary",), vmem_limit_bytes=VMEM_LIMIT),
        name="adaln",
    )(c8, w_ada, b_ada)


PROJ_ROW_CHUNK = 256


def _modulated_rmsnorm(x, gain_row, shift_row):
    r = lax.rsqrt(jnp.mean(x * x, axis=-1, keepdims=True) + EPS)
    return x * r * gain_row + shift_row


def _proj_kernel(x_ref, mod_ref, g_ref, w_hbm, lng_ref, lnb_ref, o_ref, h_scr, w_ref, stage, sem,
                 *, q_scale):
    d = x_ref.shape[1]
    pl.when(pl.program_id(0) == 0)(lambda: _load_weight_as_bf16(w_hbm, w_ref, stage, sem, axis=1))
    gain = g_ref[...] * (1.0 + mod_ref[SC_M:SC_M + 1, :])
    h_scr[...] = _modulated_rmsnorm(x_ref[...], gain, mod_ref[SH_M:SH_M + 1, :]).astype(BF16)

    def gelu_layernorm(acc):
        v = jax.nn.gelu(acc)
        mu = jnp.mean(v, axis=-1, keepdims=True)
        vc = v - mu
        var = jnp.mean(vc * vc, axis=-1, keepdims=True)
        return vc * lax.rsqrt(var + EPS) * lng_ref[...] + lnb_ref[...]

    identity = lambda acc: acc
    epilogues = (jax.nn.gelu, gelu_layernorm, lambda acc: acc * q_scale, identity, identity,
                 jax.nn.sigmoid, jax.nn.sigmoid)
    for j, epilogue in enumerate(epilogues):
        cols = slice(j * d, (j + 1) * d)
        for r in range(0, h_scr.shape[0], PROJ_ROW_CHUNK):
            rows = slice(r, r + PROJ_ROW_CHUNK)
            o_ref[rows, cols] = epilogue(_dot(h_scr[rows, :], w_ref[:, cols])).astype(BF16)


def _proj(x2d, mod3, norm_g, w_in, ln_g, ln_b, *, seq, tm):
    m, d = x2d.shape
    n = w_in.shape[1]
    tiles_per_batch = seq // tm
    kern = functools.partial(_proj_kernel, q_scale=MOBA_HEAD_DIM ** -0.5 * math.log2(math.e))
    const2 = lambda i: (0, 0)
    return pl.pallas_call(
        kern,
        grid=(m // tm,),
        in_specs=[
            pl.BlockSpec((tm, d), lambda i: (i, 0)),
            pl.BlockSpec((None, N_MOD, d), lambda i: (i // tiles_per_batch, 0, 0)),
            pl.BlockSpec((1, d), const2),
            pl.BlockSpec(memory_space=pl.ANY),
            pl.BlockSpec((1, d), const2),
            pl.BlockSpec((1, d), const2),
        ],
        out_specs=pl.BlockSpec((tm, n), lambda i: (i, 0)),
        out_shape=jax.ShapeDtypeStruct((m, n), BF16),
        scratch_shapes=[
            pltpu.VMEM((tm, d), BF16),
            pltpu.VMEM((d, n), BF16),
            pltpu.VMEM((2, d, WEIGHT_CHUNK), F32),
            pltpu.SemaphoreType.DMA((2,)),
        ],
        compiler_params=pltpu.CompilerParams(
            dimension_semantics=("arbitrary",), vmem_limit_bytes=VMEM_LIMIT),
        name="proj",
    )(x2d, mod3, norm_g, w_in, ln_g, ln_b)


ONES_ROWS = 16


def _attn_kernel(q_ref, k_ref, v_ref, o_ref, kbar_scr, vt_scr, sel_scr, s_scr, p_scr,
                 m_scr, alpha_scr, acc_scr, *, n_blocks, heads, tq):
    blk = MOBA_BLOCK
    hd = MOBA_HEAD_DIM
    qpb = tq // blk
    assert qpb == 2
    hcols = [slice(h * hd, (h + 1) * hd) for h in range(heads)]

    ones_rows = (lax.broadcasted_iota(jnp.int32, (ONES_ROWS, blk), 0) == 0).astype(F32).astype(BF16)
    for b in range(n_blocks):
        rows = slice(b * blk, (b + 1) * blk)
        kbar_scr[b:b + 1, :] = jnp.sum(k_ref[rows, :].astype(F32), axis=0, keepdims=True) * (1.0 / blk)
        for h in range(heads):
            vt_scr[b, h, :hd, :] = v_ref[rows, hcols[h]].astype(F32).T.astype(BF16)
            vt_scr[b, h, hd:, :] = ones_rows

    blk_id = lax.broadcasted_iota(jnp.int32, (n_blocks, tq), 0).astype(F32)
    col_blk = (lax.broadcasted_iota(jnp.int32, (1, tq), 1) // blk).astype(F32)
    key_pos = lax.broadcasted_iota(jnp.int32, (blk, tq), 0)
    qry_pos = lax.broadcasted_iota(jnp.int32, (blk, tq), 1)

    def q_tile(qt, carry):
        qrows = pl.ds(pl.multiple_of(qt * tq, tq), tq)
        first_blk = qt * qpb
        own_blk = lax.convert_element_type(first_blk, F32) + col_blk

        for h in range(heads):
            q = q_ref[qrows, hcols[h]]
            kb_hi, kb_lo = _split_bf16(kbar_scr[:, hcols[h]])
            gate = _dot_nt(kb_hi, q) + _dot_nt(kb_lo, q)
            g = jnp.where(blk_id < own_blk, gate, NEG)
            selected = jnp.where(blk_id == own_blk, 1.0, 0.0)
            for _ in range(MOBA_TOPK):
                gmax = jnp.max(g, axis=0, keepdims=True)
                idx = jnp.min(jnp.where(g == gmax, blk_id, float(n_blocks)), axis=0, keepdims=True)
                pick = blk_id == idx
                selected = jnp.where(pick & (idx < own_blk), 1.0, selected)
                g = jnp.where(pick, -jnp.inf, g)
            sel_scr[h] = selected
            m_scr[h] = jnp.full((1, tq), -jnp.inf, F32)
            acc_scr[h] = jnp.zeros((hd + ONES_ROWS, tq), F32)

        def scores(kb, causal_off, slot):
            krows = pl.ds(pl.multiple_of(kb * blk, blk), blk)
            cmax = []
            for h in range(heads):
                s = _dot_nt(k_ref[krows, hcols[h]], q_ref[qrows, hcols[h]])
                if causal_off is not None:
                    s = jnp.where(key_pos + causal_off <= qry_pos, s, NEG)
                s_scr[h] = s
                cmax.append(jnp.max(s, axis=0, keepdims=True))
            for h in range(heads):
                selected = sel_scr[h, pl.ds(kb, 1), :] > 0.5
                m = m_scr[h]
                m_new = jnp.where(selected, jnp.maximum(m, cmax[h]), m)
                alpha_scr[slot, h] = jnp.where(selected, jnp.exp2(m - m_new), 1.0)
                p_scr[slot, h] = jnp.exp2(s_scr[h] - jnp.where(selected, m_new, -NEG)).astype(BF16)
                m_scr[h] = m_new

        def accumulate(kb, slot):
            for h in range(heads):
                acc_scr[h] = alpha_scr[slot, h] * acc_scr[h] + _dot(vt_scr[kb, h], p_scr[slot, h])

        scores(first_blk, 0, 0)
        scores(first_blk + 1, blk, 1)
        accumulate(first_blk, 0)

        def past_pair(i, c):
            kb0 = 2 * i
            prev = jnp.where(i == 0, first_blk + 1, kb0 - 1)
            scores(kb0, None, 0)
            accumulate(prev, 1)
            scores(kb0 + 1, None, 1)
            accumulate(kb0, 0)
            return c

        lax.fori_loop(0, qt, past_pair, 0)
        accumulate(jnp.where(qt == 0, first_blk + 1, first_blk - 1), 1)

        for h in range(heads):
            acc = acc_scr[h]
            o_ref[qrows, hcols[h]] = (acc[:hd] / acc[hd:hd + 1]).T.astype(BF16)
        return carry

    lax.fori_loop(0, n_blocks // qpb, q_tile, 0)


ATTN_HEADS_PER_STEP = 4
ATTN_Q_TILE = 512


def _attention(proj, *, batch, seq, d):
    hd = MOBA_HEAD_DIM
    blk = MOBA_BLOCK
    n_blocks = seq // blk
    hg = ATTN_HEADS_PER_STEP
    tq = ATTN_Q_TILE
    groups = MOBA_HEADS // hg
    q_off, k_off, v_off = 2 * groups, 3 * groups, 4 * groups
    kern = functools.partial(_attn_kernel, n_blocks=n_blocks, heads=hg, tq=tq)
    return pl.pallas_call(
        kern,
        grid=(batch, groups),
        in_specs=[
            pl.BlockSpec((seq, hg * hd), lambda b, g: (b, q_off + g)),
            pl.BlockSpec((seq, hg * hd), lambda b, g: (b, k_off + g)),
            pl.BlockSpec((seq, hg * hd), lambda b, g: (b, v_off + g)),
        ],
        out_specs=pl.BlockSpec((seq, hg * hd), lambda b, g: (b, g)),
        out_shape=jax.ShapeDtypeStruct((batch * seq, MOBA_HEADS * hd), BF16),
        scratch_shapes=[
            pltpu.VMEM((n_blocks, hg * hd), F32),
            pltpu.VMEM((n_blocks, hg, hd + ONES_ROWS, blk), BF16),
            pltpu.VMEM((hg, n_blocks, tq), F32),
            pltpu.VMEM((hg, blk, tq), F32),
            pltpu.VMEM((2, hg, blk, tq), BF16),
            pltpu.VMEM((hg, 1, tq), F32),
            pltpu.VMEM((2, hg, 1, tq), F32),
            pltpu.VMEM((hg, hd + ONES_ROWS, tq), F32),
        ],
        compiler_params=pltpu.CompilerParams(
            dimension_semantics=("arbitrary", "arbitrary"), vmem_limit_bytes=VMEM_LIMIT),
        name="moba_attn",
    )(proj, proj, proj)


def _merge_kernel(x_ref, u_ref, v_ref, ga_ref, gb_ref, yb_ref, ws_ref, bs_ref,
                  wpa_hbm, wpb_hbm, wout_hbm, mod_ref, o_ref,
                  ya_scr, wpa_ref, wpb_ref, wout_ref, stage, sem, *, tm):
    @pl.when(pl.program_id(0) == 0)
    def _():
        for w_hbm, w_ref in ((wpa_hbm, wpa_ref), (wpb_hbm, wpb_ref), (wout_hbm, wout_ref)):
            _load_weight_as_bf16(w_hbm, w_ref, stage, sem, axis=1)

    t = GMLP_CHUNK
    r = lax.broadcasted_iota(jnp.int32, (t, t), 0)
    c = lax.broadcasted_iota(jnp.int32, (t, t), 1)
    tril = c <= r
    for g in range(GMLP_GROUPS):
        w = jnp.where(tril, ws_ref[g], 0.0).astype(BF16)
        bcol = bs_ref[:, g:g + 1]
        cols = slice(g * t, (g + 1) * t)
        for ch in range(tm // t):
            rows = slice(ch * t, (ch + 1) * t)
            mixed = _dot(w, v_ref[rows, cols]) + bcol
            ya_scr[rows, cols] = (u_ref[rows, cols].astype(F32) * mixed).astype(BF16)

    pa = _dot(ya_scr[...], wpa_ref[...])
    pb = _dot(yb_ref[...], wpb_ref[...])
    merged = ga_ref[...].astype(F32) * pa + gb_ref[...].astype(F32) * pb
    out = _dot(merged.astype(BF16), wout_ref[...])
    o_ref[...] = x_ref[...] + mod_ref[G_M:G_M + 1, :] * out


def _merge(x2d, proj, yb, w_sp, b_sp_t, wpa, wpb, wout, mod3, *, seq, tm):
    m, d = x2d.shape
    tiles_per_batch = seq // tm
    t = GMLP_CHUNK
    kern = functools.partial(_merge_kernel, tm=tm)
    const2 = lambda i: (0, 0)
    in_hbm = pl.BlockSpec(memory_space=pl.ANY)
    return pl.pallas_call(
        kern,
        grid=(m // tm,),
        in_specs=[
            pl.BlockSpec((tm, d), lambda i: (i, 0)),
            pl.BlockSpec((tm, d), lambda i: (i, 0)),
            pl.BlockSpec((tm, d), lambda i: (i, 1)),
            pl.BlockSpec((tm, d), lambda i: (i, 5)),
            pl.BlockSpec((tm, d), lambda i: (i, 6)),
            pl.BlockSpec((tm, d), lambda i: (i, 0)),
            pl.BlockSpec((GMLP_GROUPS, t, t), lambda i: (0, 0, 0)),
            pl.BlockSpec((t, GMLP_GROUPS), const2),
            in_hbm, in_hbm, in_hbm,
            pl.BlockSpec((None, N_MOD, d), lambda i: (i // tiles_per_batch, 0, 0)),
        ],
        out_specs=pl.BlockSpec((tm, d), lambda i: (i, 0)),
        out_shape=jax.ShapeDtypeStruct((m, d), F32),
        scratch_shapes=[
            pltpu.VMEM((tm, d), BF16),
            pltpu.VMEM((d, d), BF16), pltpu.VMEM((d, d), BF16), pltpu.VMEM((d, d), BF16),
            pltpu.VMEM((2, d, WEIGHT_CHUNK), F32),
            pltpu.SemaphoreType.DMA((2,)),
        ],
        compiler_params=pltpu.CompilerParams(
            dimension_semantics=("arbitrary",), vmem_limit_bytes=VMEM_LIMIT),
        name="merge",
    )(x2d, proj, proj, proj, proj, yb, w_sp, b_sp_t, wpa, wpb, wout, mod3)


FFN_ROW_CHUNK = 256


def _ffn_kernel(x_ref, mod_ref, g_ref, wg_hbm, wu_hbm, wd_hbm, gfin_ref, o_ref,
                wg_ref, wu_ref, wd_ref, stage_cols, stage_rows, sem, *, final_norm):
    @pl.when(pl.program_id(0) == 0)
    def _():
        _load_weight_as_bf16(wg_hbm, wg_ref, stage_cols, sem, axis=1)
        _load_weight_as_bf16(wu_hbm, wu_ref, stage_cols, sem, axis=1)
        _load_weight_as_bf16(wd_hbm, wd_ref, stage_rows, sem, axis=0)

    gain = g_ref[...] * (1.0 + mod_ref[SC_F:SC_F + 1, :])
    for r in range(0, x_ref.shape[0], FFN_ROW_CHUNK):
        rows = slice(r, r + FFN_ROW_CHUNK)
        x = x_ref[rows, :]
        h = _modulated_rmsnorm(x, gain, mod_ref[SH_F:SH_F + 1, :]).astype(BF16)
        gate = _dot(h, wg_ref[...])
        up = _dot(h, wu_ref[...])
        a = (gate * jax.nn.sigmoid(gate) * up).astype(BF16)
        y = x + mod_ref[G_F:G_F + 1, :] * _dot(a, wd_ref[...])
        if final_norm:
            y = y * lax.rsqrt(jnp.mean(y * y, axis=-1, keepdims=True) + EPS) * gfin_ref[...]
        o_ref[rows, :] = y


def _ffn(x2d, mod3, norm_g, wg, wu, wd, g_final, *, seq, tm, final_norm):
    m, d = x2d.shape
    fh = wg.shape[1]
    tiles_per_batch = seq // tm
    kern = functools.partial(_ffn_kernel, final_norm=final_norm)
    const2 = lambda i: (0, 0)
    in_hbm = pl.BlockSpec(memory_space=pl.ANY)
    return pl.pallas_call(
        kern,
        grid=(m // tm,),
        in_specs=[
            pl.BlockSpec((tm, d), lambda i: (i, 0)),
            pl.BlockSpec((None, N_MOD, d), lambda i: (i // tiles_per_batch, 0, 0)),
            pl.BlockSpec((1, d), const2),
            in_hbm, in_hbm, in_hbm,
            pl.BlockSpec((1, d), const2),
        ],
        out_specs=pl.BlockSpec((tm, d), lambda i: (i, 0)),
        out_shape=jax.ShapeDtypeStruct((m, d), F32),
        scratch_shapes=[
            pltpu.VMEM((d, fh), BF16), pltpu.VMEM((d, fh), BF16), pltpu.VMEM((fh, d), BF16),
            pltpu.VMEM((2, d, WEIGHT_CHUNK), F32),
            pltpu.VMEM((2, WEIGHT_CHUNK, d), F32),
            pltpu.SemaphoreType.DMA((2,)),
        ],
        compiler_params=pltpu.CompilerParams(
            dimension_semantics=("arbitrary",), vmem_limit_bytes=VMEM_LIMIT),
        name="ffn",
    )(x2d, mod3, norm_g, wg, wu, wd, g_final)


def kernel(x, c, w_ada, b_ada, norm_mix_g, w_in, ln_v_g, ln_v_b, w_spatial, b_spatial,
           w_proj_a, w_proj_b, w_out, norm_ffn_g, w_ffn_gate, w_ffn_up, w_ffn_down,
           norm_final_g):
    batch, seq, d = x.shape
    depth = w_ada.shape[0]
    assert d == MOBA_HEADS * MOBA_HEAD_DIM == GMLP_GROUPS * LANES
    assert seq % ATTN_Q_TILE == 0 and ATTN_Q_TILE % MOBA_BLOCK == 0
    assert w_in.shape[2] == N_SECTIONS * d

    x2d = x.reshape(batch * seq, d)
    c8 = jnp.pad(c, ((0, 8 - batch), (0, 0)))
    row = lambda a: a.reshape(1, -1)
    g_final = row(norm_final_g)

    for l in range(depth):
        mod = _adaln(c8, w_ada[l], row(b_ada[l]))
        mod3 = mod[:batch].reshape(batch, N_MOD, d)
        proj = _proj(x2d, mod3, row(norm_mix_g[l]), w_in[l],
                     row(ln_v_g[l]), row(ln_v_b[l]), seq=seq, tm=512)
        yb = _attention(proj, batch=batch, seq=seq, d=d)
        x2d = _merge(x2d, proj, yb, w_spatial[l], b_spatial[l].T,
                     w_proj_a[l], w_proj_b[l], w_out[l],
                     mod3, seq=seq, tm=512)
        x2d = _ffn(x2d, mod3, row(norm_ffn_g[l]), w_ffn_gate[l], w_ffn_up[l], w_ffn_down[l], g_final,
                   seq=seq, tm=512, final_norm=(l == depth - 1))
    return x2d.reshape(batch, seq, d)
```

```python
import functools
import math

import jax
import jax.numpy as jnp
from jax import lax
from jax.experimental import pallas as pl
from jax.experimental.pallas import tpu as pltpu

F32 = jnp.float32
BF16 = jnp.bfloat16

GMLP_CHUNK = 128
GMLP_GROUPS = 8
MOBA_HEADS = 8
MOBA_HEAD_DIM = 128
MOBA_BLOCK = 256
MOBA_TOPK = 3
N_MOD = 6
N_SECTIONS = 7
EPS = 1e-6
NEG = -1e30
LANES = 128
VMEM_LIMIT = 56 * 1024 * 1024

SH_M, SC_M, G_M, SH_F, SC_F, G_F = range(N_MOD)


def _dot(a, b):
    return jnp.dot(a, b, preferred_element_type=F32)


def _dot_nt(a, b):
    return lax.dot_general(a, b, (((1,), (1,)), ((), ())), preferred_element_type=F32)


def _split_bf16(a):
    hi = a.astype(BF16)
    lo = (a - hi.astype(F32)).astype(BF16)
    return hi, lo


WEIGHT_CHUNK = 256
WEIGHT_SLOTS = 4


def _load_weight_as_bf16(w_hbm, w_bf, stage, sem, *, axis):
    slots = stage.shape[0]
    chunk = stage.shape[1 + axis]
    n = w_hbm.shape[axis] // chunk
    assert n * chunk == w_hbm.shape[axis] and stage.shape[2 - axis] == w_hbm.shape[1 - axis]

    def piece(ref, c):
        idx = [slice(None), slice(None)]
        idx[axis] = pl.ds(c * chunk, chunk)
        return ref.at[tuple(idx)]

    def copy(c):
        return pltpu.make_async_copy(piece(w_hbm, c), stage.at[c % slots], sem.at[c % slots])

    for c in range(min(slots - 1, n)):
        copy(c).start()
    for c in range(n):
        if c + slots - 1 < n:
            copy(c + slots - 1).start()
        copy(c).wait()
        piece(w_bf, c)[...] = stage[c % slots].astype(BF16)


def _adaln_kernel(c_ref, w_ref, b_ref, o_ref):
    c = c_ref[...]
    rows = c.shape[0]
    c_act = c * jax.nn.sigmoid(c)
    c_hi, c_lo = _split_bf16(c_act)
    y = _dot(jnp.concatenate([c_hi, c_lo], axis=0), w_ref[...].astype(BF16))
    o_ref[...] = y[:rows] + y[rows:] + b_ref[...]


def _adaln(c8, w_ada, b_ada):
    rows, d = c8.shape
    n = w_ada.shape[1]
    tn = d
    return pl.pallas_call(
        _adaln_kernel,
        grid=(n // tn,),
        in_specs=[
            pl.BlockSpec((rows, d), lambda j: (0, 0)),
            pl.BlockSpec((d, tn), lambda j: (0, j)),
            pl.BlockSpec((1, tn), lambda j: (0, j)),
        ],
        out_specs=pl.BlockSpec((rows, tn), lambda j: (0, j)),
        out_shape=jax.ShapeDtypeStruct((rows, n), F32),
        compiler_params=pltpu.CompilerParams(
            dimension_semantics=("arbitrary",), vmem_limit_bytes=VMEM_LIMIT),
        name="adaln",
    )(c8, w_ada, b_ada)


PROJ_ROW_CHUNK = 256


def _modulated_rmsnorm(x, gain_row, shift_row):
    r = lax.rsqrt(jnp.mean(x * x, axis=-1, keepdims=True) + EPS)
    return x * r * gain_row + shift_row


def _proj_kernel(x_ref, mod_ref, g_ref, w_hbm, lng_ref, lnb_ref, o_ref, h_scr, w_ref, stage, sem,
                 *, q_scale):
    d = x_ref.shape[1]
    pl.when(pl.program_id(0) == 0)(lambda: _load_weight_as_bf16(w_hbm, w_ref, stage, sem, axis=1))
    gain = g_ref[...] * (1.0 + mod_ref[SC_M:SC_M + 1, :])
    h_scr[...] = _modulated_rmsnorm(x_ref[...], gain, mod_ref[SH_M:SH_M + 1, :]).astype(BF16)

    def gelu_layernorm(acc):
        v = jax.nn.gelu(acc)
        mu = jnp.mean(v, axis=-1, keepdims=True)
        vc = v - mu
        var = jnp.mean(vc * vc, axis=-1, keepdims=True)
        return vc * lax.rsqrt(var + EPS) * lng_ref[...] + lnb_ref[...]

    identity = lambda acc: acc
    epilogues = (jax.nn.gelu, gelu_layernorm, lambda acc: acc * q_scale, identity, identity,
                 jax.nn.sigmoid, jax.nn.sigmoid)
    for j, epilogue in enumerate(epilogues):
        cols = slice(j * d, (j + 1) * d)
        for r in range(0, h_scr.shape[0], PROJ_ROW_CHUNK):
            rows = slice(r, r + PROJ_ROW_CHUNK)
            o_ref[rows, cols] = epilogue(_dot(h_scr[rows, :], w_ref[:, cols])).astype(BF16)


def _proj(x2d, mod3, norm_g, w_in, ln_g, ln_b, *, seq, tm):
    m, d = x2d.shape
    n = w_in.shape[1]
    tiles_per_batch = seq // tm
    kern = functools.partial(_proj_kernel, q_scale=MOBA_HEAD_DIM ** -0.5 * math.log2(math.e))
    const2 = lambda i: (0, 0)
    return pl.pallas_call(
        kern,
        grid=(m // tm,),
        in_specs=[
            pl.BlockSpec((tm, d), lambda i: (i, 0)),
            pl.BlockSpec((None, N_MOD, d), lambda i: (i // tiles_per_batch, 0, 0)),
            pl.BlockSpec((1, d), const2),
            pl.BlockSpec(memory_space=pl.ANY),
            pl.BlockSpec((1, d), const2),
            pl.BlockSpec((1, d), const2),
        ],
        out_specs=pl.BlockSpec((tm, n), lambda i: (i, 0)),
        out_shape=jax.ShapeDtypeStruct((m, n), BF16),
        scratch_shapes=[
            pltpu.VMEM((tm, d), BF16),
            pltpu.VMEM((d, n), BF16),
            pltpu.VMEM((WEIGHT_SLOTS, d, WEIGHT_CHUNK), F32),
            pltpu.SemaphoreType.DMA((WEIGHT_SLOTS,)),
        ],
        compiler_params=pltpu.CompilerParams(
            dimension_semantics=("arbitrary",), vmem_limit_bytes=VMEM_LIMIT),
        name="proj",
    )(x2d, mod3, norm_g, w_in, ln_g, ln_b)


ONES_ROWS = 16


def _attn_kernel(q_ref, k_ref, v_ref, o_ref, kbar_scr, vt_scr, sel_scr, s_scr, p_scr,
                 m_scr, alpha_scr, acc_scr, *, n_blocks, heads, tq):
    blk = MOBA_BLOCK
    hd = MOBA_HEAD_DIM
    qpb = tq // blk
    assert qpb == 2
    hcols = [slice(h * hd, (h + 1) * hd) for h in range(heads)]

    ones_rows = (lax.broadcasted_iota(jnp.int32, (ONES_ROWS, blk), 0) == 0).astype(F32).astype(BF16)
    for b in range(n_blocks):
        rows = slice(b * blk, (b + 1) * blk)
        kbar_scr[b:b + 1, :] = jnp.sum(k_ref[rows, :].astype(F32), axis=0, keepdims=True) * (1.0 / blk)
        for h in range(heads):
            vt_scr[b, h, :hd, :] = v_ref[rows, hcols[h]].astype(F32).T.astype(BF16)
            vt_scr[b, h, hd:, :] = ones_rows

    blk_id = lax.broadcasted_iota(jnp.int32, (n_blocks, tq), 0).astype(F32)
    col_blk = (lax.broadcasted_iota(jnp.int32, (1, tq), 1) // blk).astype(F32)
    key_pos = lax.broadcasted_iota(jnp.int32, (blk, tq), 0)
    qry_pos = lax.broadcasted_iota(jnp.int32, (blk, tq), 1)

    def q_tile(qt, carry):
        qrows = pl.ds(pl.multiple_of(qt * tq, tq), tq)
        first_blk = qt * qpb
        own_blk = lax.convert_element_type(first_blk, F32) + col_blk

        for h in range(heads):
            q = q_ref[qrows, hcols[h]]
            kb_hi, kb_lo = _split_bf16(kbar_scr[:, hcols[h]])
            gate = _dot_nt(kb_hi, q) + _dot_nt(kb_lo, q)
            g = jnp.where(blk_id < own_blk, gate, NEG)
            selected = jnp.where(blk_id == own_blk, 1.0, 0.0)
            for _ in range(MOBA_TOPK):
                gmax = jnp.max(g, axis=0, keepdims=True)
                idx = jnp.min(jnp.where(g == gmax, blk_id, float(n_blocks)), axis=0, keepdims=True)
                pick = blk_id == idx
                selected = jnp.where(pick & (idx < own_blk), 1.0, selected)
                g = jnp.where(pick, -jnp.inf, g)
            sel_scr[h] = selected
            m_scr[h] = jnp.full((1, tq), -jnp.inf, F32)
            acc_scr[h] = jnp.zeros((hd + ONES_ROWS, tq), F32)

        def scores(kb, causal_off, slot):
            krows = pl.ds(pl.multiple_of(kb * blk, blk), blk)
            cmax = []
            for h in range(heads):
                s = _dot_nt(k_ref[krows, hcols[h]], q_ref[qrows, hcols[h]])
                if causal_off is not None:
                    s = jnp.where(key_pos + causal_off <= qry_pos, s, NEG)
                s_scr[h] = s
                cmax.append(jnp.max(s, axis=0, keepdims=True))
            for h in range(heads):
                selected = sel_scr[h, pl.ds(kb, 1), :] > 0.5
                m = m_scr[h]
                m_new = jnp.where(selected, jnp.maximum(m, cmax[h]), m)
                alpha_scr[slot, h] = jnp.where(selected, jnp.exp2(m - m_new), 1.0)
                p_scr[slot, h] = jnp.exp2(s_scr[h] - jnp.where(selected, m_new, -NEG)).astype(BF16)
                m_scr[h] = m_new

        def accumulate(kb, slot):
            for h in range(heads):
                acc_scr[h] = alpha_scr[slot, h] * acc_scr[h] + _dot(vt_scr[kb, h], p_scr[slot, h])

        scores(first_blk, 0, 0)
        scores(first_blk + 1, blk, 1)
        accumulate(first_blk, 0)

        def past_pair(i, c):
            kb0 = 2 * i
            prev = jnp.where(i == 0, first_blk + 1, kb0 - 1)
            scores(kb0, None, 0)
            accumulate(prev, 1)
            scores(kb0 + 1, None, 1)
            accumulate(kb0, 0)
            return c

        lax.fori_loop(0, qt, past_pair, 0)
        accumulate(jnp.where(qt == 0, first_blk + 1, first_blk - 1), 1)

        for h in range(heads):
            acc = acc_scr[h]
            o_ref[qrows, hcols[h]] = (acc[:hd] / acc[hd:hd + 1]).T.astype(BF16)
        return carry

    lax.fori_loop(0, n_blocks // qpb, q_tile, 0)


ATTN_HEADS_PER_STEP = 4
ATTN_Q_TILE = 512


def _attention(proj, *, batch, seq, d):
    hd = MOBA_HEAD_DIM
    blk = MOBA_BLOCK
    n_blocks = seq // blk
    hg = ATTN_HEADS_PER_STEP
    tq = ATTN_Q_TILE
    groups = MOBA_HEADS // hg
    q_off, k_off, v_off = 2 * groups, 3 * groups, 4 * groups
    kern = functools.partial(_attn_kernel, n_blocks=n_blocks, heads=hg, tq=tq)
    return pl.pallas_call(
        kern,
        grid=(batch, groups),
        in_specs=[
            pl.BlockSpec((seq, hg * hd), lambda b, g: (b, q_off + g)),
            pl.BlockSpec((seq, hg * hd), lambda b, g: (b, k_off + g)),
            pl.BlockSpec((seq, hg * hd), lambda b, g: (b, v_off + g)),
        ],
        out_specs=pl.BlockSpec((seq, hg * hd), lambda b, g: (b, g)),
        out_shape=jax.ShapeDtypeStruct((batch * seq, MOBA_HEADS * hd), BF16),
        scratch_shapes=[
            pltpu.VMEM((n_blocks, hg * hd), F32),
            pltpu.VMEM((n_blocks, hg, hd + ONES_ROWS, blk), BF16),
            pltpu.VMEM((hg, n_blocks, tq), F32),
            pltpu.VMEM((hg, blk, tq), F32),
            pltpu.VMEM((2, hg, blk, tq), BF16),
            pltpu.VMEM((hg, 1, tq), F32),
            pltpu.VMEM((2, hg, 1, tq), F32),
            pltpu.VMEM((hg, hd + ONES_ROWS, tq), F32),
        ],
        compiler_params=pltpu.CompilerParams(
            dimension_semantics=("arbitrary", "arbitrary"), vmem_limit_bytes=VMEM_LIMIT),
        name="moba_attn",
    )(proj, proj, proj)


def _merge_kernel(x_ref, u_ref, v_ref, ga_ref, gb_ref, yb_ref, ws_ref, bs_ref,
                  wpa_hbm, wpb_hbm, wout_hbm, mod_ref, o_ref,
                  ya_scr, wpa_ref, wpb_ref, wout_ref, stage, sem, *, tm):
    @pl.when(pl.program_id(0) == 0)
    def _():
        for w_hbm, w_ref in ((wpa_hbm, wpa_ref), (wpb_hbm, wpb_ref), (wout_hbm, wout_ref)):
            _load_weight_as_bf16(w_hbm, w_ref, stage, sem, axis=1)

    t = GMLP_CHUNK
    r = lax.broadcasted_iota(jnp.int32, (t, t), 0)
    c = lax.broadcasted_iota(jnp.int32, (t, t), 1)
    tril = c <= r
    for g in range(GMLP_GROUPS):
        w = jnp.where(tril, ws_ref[g], 0.0).astype(BF16)
        bcol = bs_ref[:, g:g + 1]
        cols = slice(g * t, (g + 1) * t)
        for ch in range(tm // t):
            rows = slice(ch * t, (ch + 1) * t)
            mixed = _dot(w, v_ref[rows, cols]) + bcol
            ya_scr[rows, cols] = (u_ref[rows, cols].astype(F32) * mixed).astype(BF16)

    pa = _dot(ya_scr[...], wpa_ref[...])
    pb = _dot(yb_ref[...], wpb_ref[...])
    merged = ga_ref[...].astype(F32) * pa + gb_ref[...].astype(F32) * pb
    out = _dot(merged.astype(BF16), wout_ref[...])
    o_ref[...] = x_ref[...] + mod_ref[G_M:G_M + 1, :] * out


def _merge(x2d, proj, yb, w_sp, b_sp_t, wpa, wpb, wout, mod3, *, seq, tm):
    m, d = x2d.shape
    tiles_per_batch = seq // tm
    t = GMLP_CHUNK
    kern = functools.partial(_merge_kernel, tm=tm)
    const2 = lambda i: (0, 0)
    in_hbm = pl.BlockSpec(memory_space=pl.ANY)
    return pl.pallas_call(
        kern,
        grid=(m // tm,),
        in_specs=[
            pl.BlockSpec((tm, d), lambda i: (i, 0)),
            pl.BlockSpec((tm, d), lambda i: (i, 0)),
            pl.BlockSpec((tm, d), lambda i: (i, 1)),
            pl.BlockSpec((tm, d), lambda i: (i, 5)),
            pl.BlockSpec((tm, d), lambda i: (i, 6)),
            pl.BlockSpec((tm, d), lambda i: (i, 0)),
            pl.BlockSpec((GMLP_GROUPS, t, t), lambda i: (0, 0, 0)),
            pl.BlockSpec((t, GMLP_GROUPS), const2),
            in_hbm, in_hbm, in_hbm,
            pl.BlockSpec((None, N_MOD, d), lambda i: (i // tiles_per_batch, 0, 0)),
        ],
        out_specs=pl.BlockSpec((tm, d), lambda i: (i, 0)),
        out_shape=jax.ShapeDtypeStruct((m, d), F32),
        scratch_shapes=[
            pltpu.VMEM((tm, d), BF16),
            pltpu.VMEM((d, d), BF16), pltpu.VMEM((d, d), BF16), pltpu.VMEM((d, d), BF16),
            pltpu.VMEM((WEIGHT_SLOTS, d, WEIGHT_CHUNK), F32),
            pltpu.SemaphoreType.DMA((WEIGHT_SLOTS,)),
        ],
        compiler_params=pltpu.CompilerParams(
            dimension_semantics=("arbitrary",), vmem_limit_bytes=VMEM_LIMIT),
        name="merge",
    )(x2d, proj, proj, proj, proj, yb, w_sp, b_sp_t, wpa, wpb, wout, mod3)


FFN_ROW_CHUNK = 256


def _ffn_kernel(x_ref, mod_ref, g_ref, wg_hbm, wu_hbm, wd_hbm, gfin_ref, o_ref,
                wg_ref, wu_ref, wd_ref, stage_cols, stage_rows, sem, *, final_norm):
    @pl.when(pl.program_id(0) == 0)
    def _():
        _load_weight_as_bf16(wg_hbm, wg_ref, stage_cols, sem, axis=1)
        _load_weight_as_bf16(wu_hbm, wu_ref, stage_cols, sem, axis=1)
        _load_weight_as_bf16(wd_hbm, wd_ref, stage_rows, sem, axis=0)

    gain = g_ref[...] * (1.0 + mod_ref[SC_F:SC_F + 1, :])
    for r in range(0, x_ref.shape[0], FFN_ROW_CHUNK):
        rows = slice(r, r + FFN_ROW_CHUNK)
        x = x_ref[rows, :]
        h = _modulated_rmsnorm(x, gain, mod_ref[SH_F:SH_F + 1, :]).astype(BF16)
        gate = _dot(h, wg_ref[...])
        up = _dot(h, wu_ref[...])
        a = (gate * jax.nn.sigmoid(gate) * up).astype(BF16)
        y = x + mod_ref[G_F:G_F + 1, :] * _dot(a, wd_ref[...])
        if final_norm:
            y = y * lax.rsqrt(jnp.mean(y * y, axis=-1, keepdims=True) + EPS) * gfin_ref[...]
        o_ref[rows, :] = y


def _ffn(x2d, mod3, norm_g, wg, wu, wd, g_final, *, seq, tm, final_norm):
    m, d = x2d.shape
    fh = wg.shape[1]
    tiles_per_batch = seq // tm
    kern = functools.partial(_ffn_kernel, final_norm=final_norm)
    const2 = lambda i: (0, 0)
    in_hbm = pl.BlockSpec(memory_space=pl.ANY)
    return pl.pallas_call(
        kern,
        grid=(m // tm,),
        in_specs=[
            pl.BlockSpec((tm, d), lambda i: (i, 0)),
            pl.BlockSpec((None, N_MOD, d), lambda i: (i // tiles_per_batch, 0, 0)),
            pl.BlockSpec((1, d), const2),
            in_hbm, in_hbm, in_hbm,
            pl.BlockSpec((1, d), const2),
        ],
        out_specs=pl.BlockSpec((tm, d), lambda i: (i, 0)),
        out_shape=jax.ShapeDtypeStruct((m, d), F32),
        scratch_shapes=[
            pltpu.VMEM((d, fh), BF16), pltpu.VMEM((d, fh), BF16), pltpu.VMEM((fh, d), BF16),
            pltpu.VMEM((WEIGHT_SLOTS, d, WEIGHT_CHUNK), F32),
            pltpu.VMEM((WEIGHT_SLOTS, WEIGHT_CHUNK, d), F32),
            pltpu.SemaphoreType.DMA((WEIGHT_SLOTS,)),
        ],
        compiler_params=pltpu.CompilerParams(
            dimension_semantics=("arbitrary",), vmem_limit_bytes=VMEM_LIMIT),
        name="ffn",
    )(x2d, mod3, norm_g, wg, wu, wd, g_final)


def kernel(x, c, w_ada, b_ada, norm_mix_g, w_in, ln_v_g, ln_v_b, w_spatial, b_spatial,
           w_proj_a, w_proj_b, w_out, norm_ffn_g, w_ffn_gate, w_ffn_up, w_ffn_down,
           norm_final_g):
    batch, seq, d = x.shape
    depth = w_ada.shape[0]
    assert d == MOBA_HEADS * MOBA_HEAD_DIM == GMLP_GROUPS * LANES
    assert seq % ATTN_Q_TILE == 0 and ATTN_Q_TILE % MOBA_BLOCK == 0
    assert w_in.shape[2] == N_SECTIONS * d

    x2d = x.reshape(batch * seq, d)
    c8 = jnp.pad(c, ((0, 8 - batch), (0, 0)))
    row = lambda a: a.reshape(1, -1)
    g_final = row(norm_final_g)

    for l in range(depth):
        mod = _adaln(c8, w_ada[l], row(b_ada[l]))
        mod3 = mod[:batch].reshape(batch, N_MOD, d)
        proj = _proj(x2d, mod3, row(norm_mix_g[l]), w_in[l],
                     row(ln_v_g[l]), row(ln_v_b[l]), seq=seq, tm=512)
        yb = _attention(proj, batch=batch, seq=seq, d=d)
        x2d = _merge(x2d, proj, yb, w_spatial[l], b_spatial[l].T,
                     w_proj_a[l], w_proj_b[l], w_out[l],
                     mod3, seq=seq, tm=512)
        x2d = _ffn(x2d, mod3, row(norm_ffn_g[l]), w_ffn_gate[l], w_ffn_up[l], w_ffn_down[l], g_final,
                   seq=seq, tm=512, final_norm=(l == depth - 1))
    return x2d.reshape(batch, seq, d)
```

```python
import functools
import math

import jax
import jax.numpy as jnp
from jax import lax
from jax.experimental import pallas as pl
from jax.experimental.pallas import tpu as pltpu

F32 = jnp.float32
BF16 = jnp.bfloat16

GMLP_CHUNK = 128
GMLP_GROUPS = 8
MOBA_HEADS = 8
MOBA_HEAD_DIM = 128
MOBA_BLOCK = 256
MOBA_TOPK = 3
N_MOD = 6
N_SECTIONS = 7
EPS = 1e-6
NEG = -1e30
LANES = 128
VMEM_LIMIT = 56 * 1024 * 1024

SH_M, SC_M, G_M, SH_F, SC_F, G_F = range(N_MOD)


def _dot(a, b):
    return jnp.dot(a, b, preferred_element_type=F32)


def _dot_nt(a, b):
    return lax.dot_general(a, b, (((1,), (1,)), ((), ())), preferred_element_type=F32)


def _split_bf16(a):
    hi = a.astype(BF16)
    lo = (a - hi.astype(F32)).astype(BF16)
    return hi, lo


WEIGHT_CHUNK = 256
WEIGHT_SLOTS = 4


def _load_weight_as_bf16(w_hbm, w_bf, stage, sem, *, axis):
    slots = stage.shape[0]
    chunk = stage.shape[1 + axis]
    n = w_hbm.shape[axis] // chunk
    assert n * chunk == w_hbm.shape[axis] and stage.shape[2 - axis] == w_hbm.shape[1 - axis]

    def piece(ref, c):
        idx = [slice(None), slice(None)]
        idx[axis] = pl.ds(c * chunk, chunk)
        return ref.at[tuple(idx)]

    def copy(c):
        return pltpu.make_async_copy(piece(w_hbm, c), stage.at[c % slots], sem.at[c % slots])

    for c in range(min(slots - 1, n)):
        copy(c).start()
    for c in range(n):
        if c + slots - 1 < n:
            copy(c + slots - 1).start()
        copy(c).wait()
        piece(w_bf, c)[...] = stage[c % slots].astype(BF16)


def _adaln_kernel(c_ref, w_ref, b_ref, o_ref):
    c = c_ref[...]
    rows = c.shape[0]
    c_act = c * jax.nn.sigmoid(c)
    c_hi, c_lo = _split_bf16(c_act)
    y = _dot(jnp.concatenate([c_hi, c_lo], axis=0), w_ref[...].astype(BF16))
    o_ref[...] = y[:rows] + y[rows:] + b_ref[...]


def _adaln(c8, w_ada, b_ada):
    rows, d = c8.shape
    n = w_ada.shape[1]
    tn = d
    return pl.pallas_call(
        _adaln_kernel,
        grid=(n // tn,),
        in_specs=[
            pl.BlockSpec((rows, d), lambda j: (0, 0)),
            pl.BlockSpec((d, tn), lambda j: (0, j)),
            pl.BlockSpec((1, tn), lambda j: (0, j)),
        ],
        out_specs=pl.BlockSpec((rows, tn), lambda j: (0, j)),
        out_shape=jax.ShapeDtypeStruct((rows, n), F32),
        compiler_params=pltpu.CompilerParams(
            dimension_semantics=("arbitrary",), vmem_limit_bytes=VMEM_LIMIT),
        name="adaln",
    )(c8, w_ada, b_ada)


PROJ_ROW_CHUNK = 256


def _modulated_rmsnorm(x, gain_row, shift_row):
    r = lax.rsqrt(jnp.mean(x * x, axis=-1, keepdims=True) + EPS)
    return x * r * gain_row + shift_row


def _proj_kernel(x_ref, mod_ref, g_ref, w_hbm, lng_ref, lnb_ref, o_ref, h_scr, w_ref, stage, sem,
                 *, q_scale):
    d = x_ref.shape[1]
    pl.when(pl.program_id(0) == 0)(lambda: _load_weight_as_bf16(w_hbm, w_ref, stage, sem, axis=1))
    gain = g_ref[...] * (1.0 + mod_ref[SC_M:SC_M + 1, :])
    h_scr[...] = _modulated_rmsnorm(x_ref[...], gain, mod_ref[SH_M:SH_M + 1, :]).astype(BF16)

    def gelu_layernorm(acc):
        v = jax.nn.gelu(acc)
        mu = jnp.mean(v, axis=-1, keepdims=True)
        vc = v - mu
        var = jnp.mean(vc * vc, axis=-1, keepdims=True)
        return vc * lax.rsqrt(var + EPS) * lng_ref[...] + lnb_ref[...]

    identity = lambda acc: acc
    epilogues = (jax.nn.gelu, gelu_layernorm, lambda acc: acc * q_scale, identity, identity,
                 jax.nn.sigmoid, jax.nn.sigmoid)
    for j, epilogue in enumerate(epilogues):
        cols = slice(j * d, (j + 1) * d)
        for r in range(0, h_scr.shape[0], PROJ_ROW_CHUNK):
            rows = slice(r, r + PROJ_ROW_CHUNK)
            o_ref[rows, cols] = epilogue(_dot(h_scr[rows, :], w_ref[:, cols])).astype(BF16)


def _proj(x2d, mod3, norm_g, w_in, ln_g, ln_b, *, seq, tm):
    m, d = x2d.shape
    n = w_in.shape[1]
    tiles_per_batch = seq // tm
    kern = functools.partial(_proj_kernel, q_scale=MOBA_HEAD_DIM ** -0.5 * math.log2(math.e))
    const2 = lambda i: (0, 0)
    return pl.pallas_call(
        kern,
        grid=(m // tm,),
        in_specs=[
            pl.BlockSpec((tm, d), lambda i: (i, 0)),
            pl.BlockSpec((None, N_MOD, d), lambda i: (i // tiles_per_batch, 0, 0)),
            pl.BlockSpec((1, d), const2),
            pl.BlockSpec(memory_space=pl.ANY),
            pl.BlockSpec((1, d), const2),
            pl.BlockSpec((1, d), const2),
        ],
        out_specs=pl.BlockSpec((tm, n), lambda i: (i, 0)),
        out_shape=jax.ShapeDtypeStruct((m, n), BF16),
        scratch_shapes=[
            pltpu.VMEM((tm, d), BF16),
            pltpu.VMEM((d, n), BF16),
            pltpu.VMEM((WEIGHT_SLOTS, d, WEIGHT_CHUNK), F32),
            pltpu.SemaphoreType.DMA((WEIGHT_SLOTS,)),
        ],
        compiler_params=pltpu.CompilerParams(
            dimension_semantics=("arbitrary",), vmem_limit_bytes=VMEM_LIMIT),
        name="proj",
    )(x2d, mod3, norm_g, w_in, ln_g, ln_b)


ONES_ROWS = 16


def _attn_kernel(q_ref, k_ref, v_ref, o_ref, kbar_scr, vt_scr, sel_scr, s_scr, p_scr,
                 m_scr, alpha_scr, acc_scr, *, n_blocks, heads, tq):
    blk = MOBA_BLOCK
    hd = MOBA_HEAD_DIM
    qpb = tq // blk
    assert qpb == 2
    hcols = [slice(h * hd, (h + 1) * hd) for h in range(heads)]

    ones_rows = (lax.broadcasted_iota(jnp.int32, (ONES_ROWS, blk), 0) == 0).astype(F32).astype(BF16)
    for b in range(n_blocks):
        rows = slice(b * blk, (b + 1) * blk)
        kbar_scr[b:b + 1, :] = jnp.sum(k_ref[rows, :].astype(F32), axis=0, keepdims=True) * (1.0 / blk)
        for h in range(heads):
            vt_scr[b, h, :hd, :] = v_ref[rows, hcols[h]].astype(F32).T.astype(BF16)
            vt_scr[b, h, hd:, :] = ones_rows

    blk_id = lax.broadcasted_iota(jnp.int32, (n_blocks, tq), 0).astype(F32)
    col_blk = (lax.broadcasted_iota(jnp.int32, (1, tq), 1) // blk).astype(F32)
    for h in range(heads):
        kb_hi, kb_lo = _split_bf16(kbar_scr[:, hcols[h]])
        for t in range(n_blocks // qpb):
            q = q_ref[t * tq:(t + 1) * tq, hcols[h]]
            own_blk = float(t * qpb) + col_blk
            gate = _dot_nt(kb_hi, q) + _dot_nt(kb_lo, q)
            g = jnp.where(blk_id < own_blk, gate, NEG)
            selected = jnp.where(blk_id == own_blk, 1.0, 0.0)
            for _ in range(MOBA_TOPK):
                gmax = jnp.max(g, axis=0, keepdims=True)
                idx = jnp.min(jnp.where(g == gmax, blk_id, float(n_blocks)), axis=0, keepdims=True)
                pick = blk_id == idx
                selected = jnp.where(pick & (idx < own_blk), 1.0, selected)
                g = jnp.where(pick, -jnp.inf, g)
            sel_scr[h, t] = selected

    key_pos = lax.broadcasted_iota(jnp.int32, (blk, tq), 0)
    qry_pos = lax.broadcasted_iota(jnp.int32, (blk, tq), 1)

    def accumulate(kb, slot):
        for h in range(heads):
            acc_scr[h] = alpha_scr[slot, h] * acc_scr[h] + _dot(vt_scr[kb, h], p_scr[slot, h])

    def finish_tile(t):
        accumulate(jnp.where(t == 0, 1, t * qpb - 1), 1)
        rows = pl.ds(pl.multiple_of(t * tq, tq), tq)
        for h in range(heads):
            acc = acc_scr[h]
            o_ref[rows, hcols[h]] = (acc[:hd] / acc[hd:hd + 1]).T.astype(BF16)

    def q_tile(qt, carry):
        finish_tile(jnp.maximum(qt - 1, 0))
        qrows = pl.ds(pl.multiple_of(qt * tq, tq), tq)
        first_blk = qt * qpb
        for h in range(heads):
            m_scr[h] = jnp.full((1, tq), -jnp.inf, F32)
            acc_scr[h] = jnp.zeros((hd + ONES_ROWS, tq), F32)

        def scores(kb, slot, causal_off=None):
            krows = pl.ds(pl.multiple_of(kb * blk, blk), blk)
            cmax = []
            for h in range(heads):
                s = _dot_nt(k_ref[krows, hcols[h]], q_ref[qrows, hcols[h]])
                if causal_off is not None:
                    s = jnp.where(key_pos + causal_off <= qry_pos, s, NEG)
                s_scr[h] = s
                cmax.append(jnp.max(s, axis=0, keepdims=True))
            for h in range(heads):
                selected = sel_scr[h, qt, pl.ds(kb, 1), :] > 0.5
                m = m_scr[h]
                m_new = jnp.where(selected, jnp.maximum(m, cmax[h]), m)
                alpha_scr[slot, h] = jnp.where(selected, jnp.exp2(m - m_new), 1.0)
                p_scr[slot, h] = jnp.exp2(s_scr[h] - jnp.where(selected, m_new, -NEG)).astype(BF16)
                m_scr[h] = m_new

        scores(first_blk, 0, causal_off=0)
        scores(first_blk + 1, 1, causal_off=blk)
        accumulate(first_blk, 0)

        def past_pair(i, c):
            kb0 = 2 * i
            prev = jnp.where(i == 0, first_blk + 1, kb0 - 1)
            scores(kb0, 0)
            accumulate(prev, 1)
            scores(kb0 + 1, 1)
            accumulate(kb0, 0)
            return c

        lax.fori_loop(0, qt, past_pair, 0)
        return carry

    for h in range(heads):
        alpha_scr[1, h] = jnp.ones((1, tq), F32)
        p_scr[1, h] = jnp.zeros((blk, tq), BF16)
        acc_scr[h] = jnp.ones((hd + ONES_ROWS, tq), F32)
    n_tiles = n_blocks // qpb
    lax.fori_loop(0, n_tiles, q_tile, 0)
    finish_tile(n_tiles - 1)


ATTN_HEADS_PER_STEP = 4
ATTN_Q_TILE = 512


def _attention(proj, *, batch, seq, d):
    hd = MOBA_HEAD_DIM
    blk = MOBA_BLOCK
    n_blocks = seq // blk
    hg = ATTN_HEADS_PER_STEP
    tq = ATTN_Q_TILE
    groups = MOBA_HEADS // hg
    q_off, k_off, v_off = 2 * groups, 3 * groups, 4 * groups
    kern = functools.partial(_attn_kernel, n_blocks=n_blocks, heads=hg, tq=tq)
    return pl.pallas_call(
        kern,
        grid=(batch, groups),
        in_specs=[
            pl.BlockSpec((seq, hg * hd), lambda b, g: (b, q_off + g)),
            pl.BlockSpec((seq, hg * hd), lambda b, g: (b, k_off + g)),
            pl.BlockSpec((seq, hg * hd), lambda b, g: (b, v_off + g)),
        ],
        out_specs=pl.BlockSpec((seq, hg * hd), lambda b, g: (b, g)),
        out_shape=jax.ShapeDtypeStruct((batch * seq, MOBA_HEADS * hd), BF16),
        scratch_shapes=[
            pltpu.VMEM((n_blocks, hg * hd), F32),
            pltpu.VMEM((n_blocks, hg, hd + ONES_ROWS, blk), BF16),
            pltpu.VMEM((hg, seq // tq, n_blocks, tq), F32),
            pltpu.VMEM((hg, blk, tq), F32),
            pltpu.VMEM((2, hg, blk, tq), BF16),
            pltpu.VMEM((hg, 1, tq), F32),
            pltpu.VMEM((2, hg, 1, tq), F32),
            pltpu.VMEM((hg, hd + ONES_ROWS, tq), F32),
        ],
        compiler_params=pltpu.CompilerParams(
            dimension_semantics=("arbitrary", "arbitrary"), vmem_limit_bytes=VMEM_LIMIT),
        name="moba_attn",
    )(proj, proj, proj)


def _merge_kernel(x_ref, u_ref, v_ref, ga_ref, gb_ref, yb_ref, ws_ref, bs_ref,
                  wpa_hbm, wpb_hbm, wout_hbm, mod_ref, o_ref,
                  ya_scr, wpa_ref, wpb_ref, wout_ref, stage, sem, *, tm):
    @pl.when(pl.program_id(0) == 0)
    def _():
        for w_hbm, w_ref in ((wpa_hbm, wpa_ref), (wpb_hbm, wpb_ref), (wout_hbm, wout_ref)):
            _load_weight_as_bf16(w_hbm, w_ref, stage, sem, axis=1)

    t = GMLP_CHUNK
    r = lax.broadcasted_iota(jnp.int32, (t, t), 0)
    c = lax.broadcasted_iota(jnp.int32, (t, t), 1)
    tril = c <= r
    for g in range(GMLP_GROUPS):
        w = jnp.where(tril, ws_ref[g], 0.0).astype(BF16)
        bcol = bs_ref[:, g:g + 1]
        cols = slice(g * t, (g + 1) * t)
        for ch in range(tm // t):
            rows = slice(ch * t, (ch + 1) * t)
            mixed = _dot(w, v_ref[rows, cols]) + bcol
            ya_scr[rows, cols] = (u_ref[rows, cols].astype(F32) * mixed).astype(BF16)

    pa = _dot(ya_scr[...], wpa_ref[...])
    pb = _dot(yb_ref[...], wpb_ref[...])
    merged = ga_ref[...].astype(F32) * pa + gb_ref[...].astype(F32) * pb
    out = _dot(merged.astype(BF16), wout_ref[...])
    o_ref[...] = x_ref[...] + mod_ref[G_M:G_M + 1, :] * out


def _merge(x2d, proj, yb, w_sp, b_sp_t, wpa, wpb, wout, mod3, *, seq, tm):
    m, d = x2d.shape
    tiles_per_batch = seq // tm
    t = GMLP_CHUNK
    kern = functools.partial(_merge_kernel, tm=tm)
    const2 = lambda i: (0, 0)
    in_hbm = pl.BlockSpec(memory_space=pl.ANY)
    return pl.pallas_call(
        kern,
        grid=(m // tm,),
        in_specs=[
            pl.BlockSpec((tm, d), lambda i: (i, 0)),
            pl.BlockSpec((tm, d), lambda i: (i, 0)),
            pl.BlockSpec((tm, d), lambda i: (i, 1)),
            pl.BlockSpec((tm, d), lambda i: (i, 5)),
            pl.BlockSpec((tm, d), lambda i: (i, 6)),
            pl.BlockSpec((tm, d), lambda i: (i, 0)),
            pl.BlockSpec((GMLP_GROUPS, t, t), lambda i: (0, 0, 0)),
            pl.BlockSpec((t, GMLP_GROUPS), const2),
            in_hbm, in_hbm, in_hbm,
            pl.BlockSpec((None, N_MOD, d), lambda i: (i // tiles_per_batch, 0, 0)),
        ],
        out_specs=pl.BlockSpec((tm, d), lambda i: (i, 0)),
        out_shape=jax.ShapeDtypeStruct((m, d), F32),
        scratch_shapes=[
            pltpu.VMEM((tm, d), BF16),
            pltpu.VMEM((d, d), BF16), pltpu.VMEM((d, d), BF16), pltpu.VMEM((d, d), BF16),
            pltpu.VMEM((WEIGHT_SLOTS, d, WEIGHT_CHUNK), F32),
            pltpu.SemaphoreType.DMA((WEIGHT_SLOTS,)),
        ],
        compiler_params=pltpu.CompilerParams(
            dimension_semantics=("arbitrary",), vmem_limit_bytes=VMEM_LIMIT),
        name="merge",
    )(x2d, proj, proj, proj, proj, yb, w_sp, b_sp_t, wpa, wpb, wout, mod3)


FFN_ROW_CHUNK = 256


def _ffn_kernel(x_ref, mod_ref, g_ref, wg_hbm, wu_hbm, wd_hbm, gfin_ref, o_ref,
                wg_ref, wu_ref, wd_ref, stage_cols, stage_rows, sem, *, final_norm):
    @pl.when(pl.program_id(0) == 0)
    def _():
        _load_weight_as_bf16(wg_hbm, wg_ref, stage_cols, sem, axis=1)
        _load_weight_as_bf16(wu_hbm, wu_ref, stage_cols, sem, axis=1)
        _load_weight_as_bf16(wd_hbm, wd_ref, stage_rows, sem, axis=0)

    gain = g_ref[...] * (1.0 + mod_ref[SC_F:SC_F + 1, :])
    for r in range(0, x_ref.shape[0], FFN_ROW_CHUNK):
        rows = slice(r, r + FFN_ROW_CHUNK)
        x = x_ref[rows, :]
        h = _modulated_rmsnorm(x, gain, mod_ref[SH_F:SH_F + 1, :]).astype(BF16)
        gate = _dot(h, wg_ref[...])
        up = _dot(h, wu_ref[...])
        a = (gate * jax.nn.sigmoid(gate) * up).astype(BF16)
        y = x + mod_ref[G_F:G_F + 1, :] * _dot(a, wd_ref[...])
        if final_norm:
            y = y * lax.rsqrt(jnp.mean(y * y, axis=-1, keepdims=True) + EPS) * gfin_ref[...]
        o_ref[rows, :] = y


def _ffn(x2d, mod3, norm_g, wg, wu, wd, g_final, *, seq, tm, final_norm):
    m, d = x2d.shape
    fh = wg.shape[1]
    tiles_per_batch = seq // tm
    kern = functools.partial(_ffn_kernel, final_norm=final_norm)
    const2 = lambda i: (0, 0)
    in_hbm = pl.BlockSpec(memory_space=pl.ANY)
    return pl.pallas_call(
        kern,
        grid=(m // tm,),
        in_specs=[
            pl.BlockSpec((tm, d), lambda i: (i, 0)),
            pl.BlockSpec((None, N_MOD, d), lambda i: (i // tiles_per_batch, 0, 0)),
            pl.BlockSpec((1, d), const2),
            in_hbm, in_hbm, in_hbm,
            pl.BlockSpec((1, d), const2),
        ],
        out_specs=pl.BlockSpec((tm, d), lambda i: (i, 0)),
        out_shape=jax.ShapeDtypeStruct((m, d), F32),
        scratch_shapes=[
            pltpu.VMEM((d, fh), BF16), pltpu.VMEM((d, fh), BF16), pltpu.VMEM((fh, d), BF16),
            pltpu.VMEM((WEIGHT_SLOTS, d, WEIGHT_CHUNK), F32),
            pltpu.VMEM((WEIGHT_SLOTS, WEIGHT_CHUNK, d), F32),
            pltpu.SemaphoreType.DMA((WEIGHT_SLOTS,)),
        ],
        compiler_params=pltpu.CompilerParams(
            dimension_semantics=("arbitrary",), vmem_limit_bytes=VMEM_LIMIT),
        name="ffn",
    )(x2d, mod3, norm_g, wg, wu, wd, g_final)


def kernel(x, c, w_ada, b_ada, norm_mix_g, w_in, ln_v_g, ln_v_b, w_spatial, b_spatial,
           w_proj_a, w_proj_b, w_out, norm_ffn_g, w_ffn_gate, w_ffn_up, w_ffn_down,
           norm_final_g):
    batch, seq, d = x.shape
    depth = w_ada.shape[0]
    assert d == MOBA_HEADS * MOBA_HEAD_DIM == GMLP_GROUPS * LANES
    assert seq % ATTN_Q_TILE == 0 and ATTN_Q_TILE % MOBA_BLOCK == 0
    assert w_in.shape[2] == N_SECTIONS * d

    x2d = x.reshape(batch * seq, d)
    c8 = jnp.pad(c, ((0, 8 - batch), (0, 0)))
    row = lambda a: a.reshape(1, -1)
    g_final = row(norm_final_g)

    for l in range(depth):
        mod = _adaln(c8, w_ada[l], row(b_ada[l]))
        mod3 = mod[:batch].reshape(batch, N_MOD, d)
        proj = _proj(x2d, mod3, row(norm_mix_g[l]), w_in[l],
                     row(ln_v_g[l]), row(ln_v_b[l]), seq=seq, tm=512)
        yb = _attention(proj, batch=batch, seq=seq, d=d)
        x2d = _merge(x2d, proj, yb, w_spatial[l], b_spatial[l].T,
                     w_proj_a[l], w_proj_b[l], w_out[l],
                     mod3, seq=seq, tm=512)
        x2d = _ffn(x2d, mod3, row(norm_ffn_g[l]), w_ffn_gate[l], w_ffn_up[l], w_ffn_down[l], g_final,
                   seq=seq, tm=512, final_norm=(l == depth - 1))
    return x2d.reshape(batch, seq, d)
```

```python
import functools
import math

import jax
import jax.numpy as jnp
from jax import lax
from jax.experimental import pallas as pl
from jax.experimental.pallas import tpu as pltpu

F32 = jnp.float32
BF16 = jnp.bfloat16

GMLP_CHUNK = 128
GMLP_GROUPS = 8
MOBA_HEADS = 8
MOBA_HEAD_DIM = 128
MOBA_BLOCK = 256
MOBA_TOPK = 3
N_MOD = 6
N_SECTIONS = 7
EPS = 1e-6
NEG = -1e30
LANES = 128
VMEM_LIMIT = 56 * 1024 * 1024

SH_M, SC_M, G_M, SH_F, SC_F, G_F = range(N_MOD)


def _dot(a, b):
    return jnp.dot(a, b, preferred_element_type=F32)


def _dot_nt(a, b):
    return lax.dot_general(a, b, (((1,), (1,)), ((), ())), preferred_element_type=F32)


def _split_bf16(a):
    hi = a.astype(BF16)
    lo = (a - hi.astype(F32)).astype(BF16)
    return hi, lo


WEIGHT_CHUNK = 256
WEIGHT_SLOTS = 8


def _load_weight_as_bf16(w_hbm, w_bf, stage, sem, *, axis):
    slots = stage.shape[0]
    chunk = stage.shape[1 + axis]
    n = w_hbm.shape[axis] // chunk
    assert n * chunk == w_hbm.shape[axis] and stage.shape[2 - axis] == w_hbm.shape[1 - axis]

    def piece(ref, c):
        idx = [slice(None), slice(None)]
        idx[axis] = pl.ds(c * chunk, chunk)
        return ref.at[tuple(idx)]

    def copy(c):
        return pltpu.make_async_copy(piece(w_hbm, c), stage.at[c % slots], sem.at[c % slots])

    for c in range(min(slots - 1, n)):
        copy(c).start()
    for c in range(n):
        if c + slots - 1 < n:
            copy(c + slots - 1).start()
        copy(c).wait()
        piece(w_bf, c)[...] = stage[c % slots].astype(BF16)


def _adaln_kernel(c_ref, w_ref, b_ref, o_ref):
    c = c_ref[...]
    rows = c.shape[0]
    c_act = c * jax.nn.sigmoid(c)
    stacked = jnp.concatenate([c_act, c_act], axis=0)
    hi, lo = _split_bf16(stacked)
    upper = lax.broadcasted_iota(jnp.int32, stacked.shape, 0) < rows
    y = _dot(jnp.where(upper, hi, lo), w_ref[...].astype(BF16))
    o_ref[...] = y[:rows] + y[rows:] + b_ref[...]


def _adaln(c, w_ada, b_ada):
    rows, d = c.shape
    n = w_ada.shape[1]
    tn = d
    return pl.pallas_call(
        _adaln_kernel,
        grid=(n // tn,),
        in_specs=[
            pl.BlockSpec((rows, d), lambda j: (0, 0)),
            pl.BlockSpec((d, tn), lambda j: (0, j)),
            pl.BlockSpec((1, tn), lambda j: (0, j)),
        ],
        out_specs=pl.BlockSpec((rows, tn), lambda j: (0, j)),
        out_shape=jax.ShapeDtypeStruct((rows, n), F32),
        compiler_params=pltpu.CompilerParams(
            dimension_semantics=("arbitrary",), vmem_limit_bytes=VMEM_LIMIT),
        name="adaln",
    )(c, w_ada, b_ada)


PROJ_ROW_CHUNK = 256


def _modulated_rmsnorm(x, gain_row, shift_row):
    r = lax.rsqrt(jnp.mean(x * x, axis=-1, keepdims=True) + EPS)
    return x * r * gain_row + shift_row


def _proj_kernel(x_ref, mod_ref, g_ref, w_hbm, lng_ref, lnb_ref, o_ref, h_scr, w_ref, stage, sem,
                 *, q_scale):
    d = x_ref.shape[1]
    pl.when(pl.program_id(0) == 0)(lambda: _load_weight_as_bf16(w_hbm, w_ref, stage, sem, axis=1))
    gain = g_ref[...] * (1.0 + mod_ref[SC_M:SC_M + 1, :])
    h_scr[...] = _modulated_rmsnorm(x_ref[...], gain, mod_ref[SH_M:SH_M + 1, :]).astype(BF16)

    def gelu_layernorm(acc):
        v = jax.nn.gelu(acc)
        mu = jnp.mean(v, axis=-1, keepdims=True)
        vc = v - mu
        var = jnp.mean(vc * vc, axis=-1, keepdims=True)
        return vc * lax.rsqrt(var + EPS) * lng_ref[...] + lnb_ref[...]

    identity = lambda acc: acc
    epilogues = (jax.nn.gelu, gelu_layernorm, lambda acc: acc * q_scale, identity, identity,
                 jax.nn.sigmoid, jax.nn.sigmoid)
    for j, epilogue in enumerate(epilogues):
        cols = slice(j * d, (j + 1) * d)
        for r in range(0, h_scr.shape[0], PROJ_ROW_CHUNK):
            rows = slice(r, r + PROJ_ROW_CHUNK)
            o_ref[rows, cols] = epilogue(_dot(h_scr[rows, :], w_ref[:, cols])).astype(BF16)


def _proj(x2d, mod3, norm_g, w_in, ln_g, ln_b, *, seq, tm):
    m, d = x2d.shape
    n = w_in.shape[1]
    tiles_per_batch = seq // tm
    kern = functools.partial(_proj_kernel, q_scale=MOBA_HEAD_DIM ** -0.5 * math.log2(math.e))
    const2 = lambda i: (0, 0)
    return pl.pallas_call(
        kern,
        grid=(m // tm,),
        in_specs=[
            pl.BlockSpec((tm, d), lambda i: (i, 0)),
            pl.BlockSpec((None, N_MOD, d), lambda i: (i // tiles_per_batch, 0, 0)),
            pl.BlockSpec((1, d), const2),
            pl.BlockSpec(memory_space=pl.ANY),
            pl.BlockSpec((1, d), const2),
            pl.BlockSpec((1, d), const2),
        ],
        out_specs=pl.BlockSpec((tm, n), lambda i: (i, 0)),
        out_shape=jax.ShapeDtypeStruct((m, n), BF16),
        scratch_shapes=[
            pltpu.VMEM((tm, d), BF16),
            pltpu.VMEM((d, n), BF16),
            pltpu.VMEM((WEIGHT_SLOTS, d, WEIGHT_CHUNK), F32),
            pltpu.SemaphoreType.DMA((WEIGHT_SLOTS,)),
        ],
        compiler_params=pltpu.CompilerParams(
            dimension_semantics=("arbitrary",), vmem_limit_bytes=VMEM_LIMIT),
        name="proj",
    )(x2d, mod3, norm_g, w_in, ln_g, ln_b)


ONES_ROWS = 16


def _attn_kernel(q_ref, k_ref, v_ref, o_ref, kbar_scr, vt_scr, sel_scr, s_scr, p_scr,
                 m_scr, alpha_scr, acc_scr, *, n_blocks, heads, tq):
    blk = MOBA_BLOCK
    hd = MOBA_HEAD_DIM
    qpb = tq // blk
    assert qpb == 2
    hcols = [slice(h * hd, (h + 1) * hd) for h in range(heads)]

    ones_rows = (lax.broadcasted_iota(jnp.int32, (ONES_ROWS, blk), 0) == 0).astype(F32).astype(BF16)
    for b in range(n_blocks):
        rows = slice(b * blk, (b + 1) * blk)
        kbar_scr[b:b + 1, :] = jnp.sum(k_ref[rows, :].astype(F32), axis=0, keepdims=True) * (1.0 / blk)
        for h in range(heads):
            vt_scr[b, h, :hd, :] = v_ref[rows, hcols[h]].astype(F32).T.astype(BF16)
            vt_scr[b, h, hd:, :] = ones_rows

    blk_id = lax.broadcasted_iota(jnp.int32, (n_blocks, tq), 0).astype(F32)
    col_blk = (lax.broadcasted_iota(jnp.int32, (1, tq), 1) // blk).astype(F32)
    for h in range(heads):
        kb_hi, kb_lo = _split_bf16(kbar_scr[:, hcols[h]])
        for t in range(n_blocks // qpb):
            q = q_ref[t * tq:(t + 1) * tq, hcols[h]]
            own_blk = float(t * qpb) + col_blk
            gate = _dot_nt(kb_hi, q) + _dot_nt(kb_lo, q)
            g = jnp.where(blk_id < own_blk, gate, NEG)
            selected = jnp.where(blk_id == own_blk, 1.0, 0.0)
            for _ in range(MOBA_TOPK):
                gmax = jnp.max(g, axis=0, keepdims=True)
                idx = jnp.min(jnp.where(g == gmax, blk_id, float(n_blocks)), axis=0, keepdims=True)
                pick = blk_id == idx
                selected = jnp.where(pick & (idx < own_blk), 1.0, selected)
                g = jnp.where(pick, -jnp.inf, g)
            sel_scr[h, t] = selected

    key_pos = lax.broadcasted_iota(jnp.int32, (blk, tq), 0)
    qry_pos = lax.broadcasted_iota(jnp.int32, (blk, tq), 1)

    def accumulate(kb, slot):
        for h in range(heads):
            acc_scr[h] = alpha_scr[slot, h] * acc_scr[h] + _dot(vt_scr[kb, h], p_scr[slot, h])

    def finish_tile(t):
        accumulate(jnp.where(t == 0, 1, t * qpb - 1), 1)
        rows = pl.ds(pl.multiple_of(t * tq, tq), tq)
        for h in range(heads):
            acc = acc_scr[h]
            o_ref[rows, hcols[h]] = (acc[:hd] / acc[hd:hd + 1]).T.astype(BF16)

    def q_tile(qt, carry):
        finish_tile(jnp.maximum(qt - 1, 0))
        qrows = pl.ds(pl.multiple_of(qt * tq, tq), tq)
        first_blk = qt * qpb
        for h in range(heads):
            m_scr[h] = jnp.full((1, tq), -jnp.inf, F32)
            acc_scr[h] = jnp.zeros((hd + ONES_ROWS, tq), F32)

        def scores(kb, slot, causal_off=None):
            krows = pl.ds(pl.multiple_of(kb * blk, blk), blk)
            cmax = []
            for h in range(heads):
                s = _dot_nt(k_ref[krows, hcols[h]], q_ref[qrows, hcols[h]])
                if causal_off is not None:
                    s = jnp.where(key_pos + causal_off <= qry_pos, s, NEG)
                s_scr[h] = s
                cmax.append(jnp.max(s, axis=0, keepdims=True))
            for h in range(heads):
                selected = sel_scr[h, qt, pl.ds(kb, 1), :] > 0.5
                m = m_scr[h]
                m_new = jnp.where(selected, jnp.maximum(m, cmax[h]), m)
                alpha_scr[slot, h] = jnp.where(selected, jnp.exp2(m - m_new), 1.0)
                p_scr[slot, h] = jnp.exp2(s_scr[h] - jnp.where(selected, m_new, -NEG)).astype(BF16)
                m_scr[h] = m_new

        scores(first_blk, 0, causal_off=0)
        scores(first_blk + 1, 1, causal_off=blk)
        accumulate(first_blk, 0)

        def past_pair(i, c):
            kb0 = 2 * i
            prev = jnp.where(i == 0, first_blk + 1, kb0 - 1)
            scores(kb0, 0)
            accumulate(prev, 1)
            scores(kb0 + 1, 1)
            accumulate(kb0, 0)
            return c

        lax.fori_loop(0, qt, past_pair, 0)
        return carry

    for h in range(heads):
        alpha_scr[1, h] = jnp.ones((1, tq), F32)
        p_scr[1, h] = jnp.zeros((blk, tq), BF16)
        acc_scr[h] = jnp.ones((hd + ONES_ROWS, tq), F32)
    n_tiles = n_blocks // qpb
    lax.fori_loop(0, n_tiles, q_tile, 0)
    finish_tile(n_tiles - 1)


ATTN_HEADS_PER_STEP = 4
ATTN_Q_TILE = 512


def _attention(proj, *, batch, seq, d):
    hd = MOBA_HEAD_DIM
    blk = MOBA_BLOCK
    n_blocks = seq // blk
    hg = ATTN_HEADS_PER_STEP
    tq = ATTN_Q_TILE
    groups = MOBA_HEADS // hg
    q_off, k_off, v_off = 2 * groups, 3 * groups, 4 * groups
    kern = functools.partial(_attn_kernel, n_blocks=n_blocks, heads=hg, tq=tq)
    return pl.pallas_call(
        kern,
        grid=(batch, groups),
        in_specs=[
            pl.BlockSpec((seq, hg * hd), lambda b, g: (b, q_off + g)),
            pl.BlockSpec((seq, hg * hd), lambda b, g: (b, k_off + g)),
            pl.BlockSpec((seq, hg * hd), lambda b, g: (b, v_off + g)),
        ],
        out_specs=pl.BlockSpec((seq, hg * hd), lambda b, g: (b, g)),
        out_shape=jax.ShapeDtypeStruct((batch * seq, MOBA_HEADS * hd), BF16),
        scratch_shapes=[
            pltpu.VMEM((n_blocks, hg * hd), F32),
            pltpu.VMEM((n_blocks, hg, hd + ONES_ROWS, blk), BF16),
            pltpu.VMEM((hg, seq // tq, n_blocks, tq), F32),
            pltpu.VMEM((hg, blk, tq), F32),
            pltpu.VMEM((2, hg, blk, tq), BF16),
            pltpu.VMEM((hg, 1, tq), F32),
            pltpu.VMEM((2, hg, 1, tq), F32),
            pltpu.VMEM((hg, hd + ONES_ROWS, tq), F32),
        ],
        compiler_params=pltpu.CompilerParams(
            dimension_semantics=("arbitrary", "arbitrary"), vmem_limit_bytes=VMEM_LIMIT),
        name="moba_attn",
    )(proj, proj, proj)


def _merge_kernel(x_ref, u_ref, v_ref, ga_ref, gb_ref, yb_ref, ws_ref, bs_ref,
                  wpa_hbm, wpb_hbm, wout_hbm, mod_ref, o_ref,
                  ya_scr, wpa_ref, wpb_ref, wout_ref, stage, sem, *, tm):
    @pl.when(pl.program_id(0) == 0)
    def _():
        for w_hbm, w_ref in ((wpa_hbm, wpa_ref), (wpb_hbm, wpb_ref), (wout_hbm, wout_ref)):
            _load_weight_as_bf16(w_hbm, w_ref, stage, sem, axis=1)

    t = GMLP_CHUNK
    r = lax.broadcasted_iota(jnp.int32, (t, t), 0)
    c = lax.broadcasted_iota(jnp.int32, (t, t), 1)
    tril = c <= r
    for g in range(GMLP_GROUPS):
        w = jnp.where(tril, ws_ref[g], 0.0).astype(BF16)
        bcol = bs_ref[:, g:g + 1]
        cols = slice(g * t, (g + 1) * t)
        for ch in range(tm // t):
            rows = slice(ch * t, (ch + 1) * t)
            mixed = _dot(w, v_ref[rows, cols]) + bcol
            ya_scr[rows, cols] = (u_ref[rows, cols].astype(F32) * mixed).astype(BF16)

    pa = _dot(ya_scr[...], wpa_ref[...])
    pb = _dot(yb_ref[...], wpb_ref[...])
    merged = ga_ref[...].astype(F32) * pa + gb_ref[...].astype(F32) * pb
    out = _dot(merged.astype(BF16), wout_ref[...])
    o_ref[...] = x_ref[...] + mod_ref[G_M:G_M + 1, :] * out


def _merge(x2d, proj, yb, w_sp, b_sp_t, wpa, wpb, wout, mod3, *, seq, tm):
    m, d = x2d.shape
    tiles_per_batch = seq // tm
    t = GMLP_CHUNK
    kern = functools.partial(_merge_kernel, tm=tm)
    const2 = lambda i: (0, 0)
    in_hbm = pl.BlockSpec(memory_space=pl.ANY)
    return pl.pallas_call(
        kern,
        grid=(m // tm,),
        in_specs=[
            pl.BlockSpec((tm, d), lambda i: (i, 0)),
            pl.BlockSpec((tm, d), lambda i: (i, 0)),
            pl.BlockSpec((tm, d), lambda i: (i, 1)),
            pl.BlockSpec((tm, d), lambda i: (i, 5)),
            pl.BlockSpec((tm, d), lambda i: (i, 6)),
            pl.BlockSpec((tm, d), lambda i: (i, 0)),
            pl.BlockSpec((GMLP_GROUPS, t, t), lambda i: (0, 0, 0)),
            pl.BlockSpec((t, GMLP_GROUPS), const2),
            in_hbm, in_hbm, in_hbm,
            pl.BlockSpec((None, N_MOD, d), lambda i: (i // tiles_per_batch, 0, 0)),
        ],
        out_specs=pl.BlockSpec((tm, d), lambda i: (i, 0)),
        out_shape=jax.ShapeDtypeStruct((m, d), F32),
        scratch_shapes=[
            pltpu.VMEM((tm, d), BF16),
            pltpu.VMEM((d, d), BF16), pltpu.VMEM((d, d), BF16), pltpu.VMEM((d, d), BF16),
            pltpu.VMEM((WEIGHT_SLOTS, d, WEIGHT_CHUNK), F32),
            pltpu.SemaphoreType.DMA((WEIGHT_SLOTS,)),
        ],
        compiler_params=pltpu.CompilerParams(
            dimension_semantics=("arbitrary",), vmem_limit_bytes=VMEM_LIMIT),
        name="merge",
    )(x2d, proj, proj, proj, proj, yb, w_sp, b_sp_t, wpa, wpb, wout, mod3)


FFN_ROW_CHUNK = 256


def _ffn_kernel(x_ref, mod_ref, g_ref, wg_hbm, wu_hbm, wd_hbm, gfin_ref, o_ref,
                wg_ref, wu_ref, wd_ref, stage_cols, stage_rows, sem, *, final_norm):
    @pl.when(pl.program_id(0) == 0)
    def _():
        _load_weight_as_bf16(wg_hbm, wg_ref, stage_cols, sem, axis=1)
        _load_weight_as_bf16(wu_hbm, wu_ref, stage_cols, sem, axis=1)
        _load_weight_as_bf16(wd_hbm, wd_ref, stage_rows, sem, axis=0)

    gain = g_ref[...] * (1.0 + mod_ref[SC_F:SC_F + 1, :])
    for r in range(0, x_ref.shape[0], FFN_ROW_CHUNK):
        rows = slice(r, r + FFN_ROW_CHUNK)
        x = x_ref[rows, :]
        h = _modulated_rmsnorm(x, gain, mod_ref[SH_F:SH_F + 1, :]).astype(BF16)
        gate = _dot(h, wg_ref[...])
        up = _dot(h, wu_ref[...])
        a = (gate * jax.nn.sigmoid(gate) * up).astype(BF16)
        y = x + mod_ref[G_F:G_F + 1, :] * _dot(a, wd_ref[...])
        if final_norm:
            y = y * lax.rsqrt(jnp.mean(y * y, axis=-1, keepdims=True) + EPS) * gfin_ref[...]
        o_ref[rows, :] = y


def _ffn(x2d, mod3, norm_g, wg, wu, wd, g_final, *, seq, tm, final_norm):
    m, d = x2d.shape
    fh = wg.shape[1]
    tiles_per_batch = seq // tm
    kern = functools.partial(_ffn_kernel, final_norm=final_norm)
    const2 = lambda i: (0, 0)
    in_hbm = pl.BlockSpec(memory_space=pl.ANY)
    return pl.pallas_call(
        kern,
        grid=(m // tm,),
        in_specs=[
            pl.BlockSpec((tm, d), lambda i: (i, 0)),
            pl.BlockSpec((None, N_MOD, d), lambda i: (i // tiles_per_batch, 0, 0)),
            pl.BlockSpec((1, d), const2),
            in_hbm, in_hbm, in_hbm,
            pl.BlockSpec((1, d), const2),
        ],
        out_specs=pl.BlockSpec((tm, d), lambda i: (i, 0)),
        out_shape=jax.ShapeDtypeStruct((m, d), F32),
        scratch_shapes=[
            pltpu.VMEM((d, fh), BF16), pltpu.VMEM((d, fh), BF16), pltpu.VMEM((fh, d), BF16),
            pltpu.VMEM((WEIGHT_SLOTS, d, WEIGHT_CHUNK), F32),
            pltpu.VMEM((WEIGHT_SLOTS, WEIGHT_CHUNK, d), F32),
            pltpu.SemaphoreType.DMA((WEIGHT_SLOTS,)),
        ],
        compiler_params=pltpu.CompilerParams(
            dimension_semantics=("arbitrary",), vmem_limit_bytes=VMEM_LIMIT),
        name="ffn",
    )(x2d, mod3, norm_g, wg, wu, wd, g_final)


def kernel(x, c, w_ada, b_ada, norm_mix_g, w_in, ln_v_g, ln_v_b, w_spatial, b_spatial,
           w_proj_a, w_proj_b, w_out, norm_ffn_g, w_ffn_gate, w_ffn_up, w_ffn_down,
           norm_final_g):
    batch, seq, d = x.shape
    depth = w_ada.shape[0]
    assert d == MOBA_HEADS * MOBA_HEAD_DIM == GMLP_GROUPS * LANES
    assert seq % ATTN_Q_TILE == 0 and ATTN_Q_TILE % MOBA_BLOCK == 0
    assert w_in.shape[2] == N_SECTIONS * d

    x2d = x.reshape(batch * seq, d)
    row = lambda a: a.reshape(1, -1)
    g_final = row(norm_final_g)

    for l in range(depth):
        mod3 = _adaln(c, w_ada[l], row(b_ada[l])).reshape(batch, N_MOD, d)
        proj = _proj(x2d, mod3, row(norm_mix_g[l]), w_in[l],
                     row(ln_v_g[l]), row(ln_v_b[l]), seq=seq, tm=512)
        yb = _attention(proj, batch=batch, seq=seq, d=d)
        x2d = _merge(x2d, proj, yb, w_spatial[l], b_spatial[l].T,
                     w_proj_a[l], w_proj_b[l], w_out[l],
                     mod3, seq=seq, tm=512)
        x2d = _ffn(x2d, mod3, row(norm_ffn_g[l]), w_ffn_gate[l], w_ffn_up[l], w_ffn_down[l], g_final,
                   seq=seq, tm=512, final_norm=(l == depth - 1))
    return x2d.reshape(batch, seq, d)
```

```python
import functools
import math

import jax
import jax.numpy as jnp
from jax import lax
from jax.experimental import pallas as pl
from jax.experimental.pallas import tpu as pltpu

F32 = jnp.float32
BF16 = jnp.bfloat16

GMLP_CHUNK = 128
GMLP_GROUPS = 8
MOBA_HEADS = 8
MOBA_HEAD_DIM = 128
MOBA_BLOCK = 256
MOBA_TOPK = 3
N_MOD = 6
N_SECTIONS = 7
EPS = 1e-6
NEG = -1e30
LANES = 128
VMEM_LIMIT = 56 * 1024 * 1024

SH_M, SC_M, G_M, SH_F, SC_F, G_F = range(N_MOD)


def _dot(a, b):
    return jnp.dot(a, b, preferred_element_type=F32)


def _dot_nt(a, b):
    return lax.dot_general(a, b, (((1,), (1,)), ((), ())), preferred_element_type=F32)


def _split_bf16(a):
    hi = a.astype(BF16)
    lo = (a - hi.astype(F32)).astype(BF16)
    return hi, lo


WEIGHT_CHUNK = 256
WEIGHT_SLOTS = 8


def _load_weight_as_bf16(w_hbm, w_bf, stage, sem, *, axis):
    slots = stage.shape[0]
    chunk = stage.shape[1 + axis]
    n = w_hbm.shape[axis] // chunk
    assert n * chunk == w_hbm.shape[axis] and stage.shape[2 - axis] == w_hbm.shape[1 - axis]

    def piece(ref, c):
        idx = [slice(None), slice(None)]
        idx[axis] = pl.ds(c * chunk, chunk)
        return ref.at[tuple(idx)]

    def copy(c):
        return pltpu.make_async_copy(piece(w_hbm, c), stage.at[c % slots], sem.at[c % slots])

    for c in range(min(slots - 1, n)):
        copy(c).start()
    for c in range(n):
        if c + slots - 1 < n:
            copy(c + slots - 1).start()
        copy(c).wait()
        piece(w_bf, c)[...] = stage[c % slots].astype(BF16)


def _adaln_kernel(c_ref, w_ref, b_ref, o_ref):
    c = c_ref[...]
    rows = c.shape[0]
    c_act = c * jax.nn.sigmoid(c)
    stacked = jnp.concatenate([c_act, c_act], axis=0)
    hi, lo = _split_bf16(stacked)
    upper = lax.broadcasted_iota(jnp.int32, stacked.shape, 0) < rows
    y = _dot(jnp.where(upper, hi, lo), w_ref[...].astype(BF16))
    o_ref[...] = y[:rows] + y[rows:] + b_ref[...]


def _adaln(c, w_ada, b_ada):
    rows, d = c.shape
    n = w_ada.shape[1]
    tn = d
    return pl.pallas_call(
        _adaln_kernel,
        grid=(n // tn,),
        in_specs=[
            pl.BlockSpec((rows, d), lambda j: (0, 0)),
            pl.BlockSpec((d, tn), lambda j: (0, j)),
            pl.BlockSpec((1, tn), lambda j: (0, j)),
        ],
        out_specs=pl.BlockSpec((rows, tn), lambda j: (0, j)),
        out_shape=jax.ShapeDtypeStruct((rows, n), F32),
        compiler_params=pltpu.CompilerParams(
            dimension_semantics=("arbitrary",), vmem_limit_bytes=VMEM_LIMIT),
        name="adaln",
    )(c, w_ada, b_ada)


PROJ_ROW_CHUNK = 256


def _modulated_rmsnorm(x, gain_row, shift_row):
    r = lax.rsqrt(jnp.mean(x * x, axis=-1, keepdims=True) + EPS)
    return x * r * gain_row + shift_row


def _proj_kernel(x_ref, mod_ref, g_ref, w_hbm, lng_ref, lnb_ref, o_ref, h_scr, w_ref, stage, sem,
                 *, q_scale):
    d = x_ref.shape[1]
    pl.when(pl.program_id(0) == 0)(lambda: _load_weight_as_bf16(w_hbm, w_ref, stage, sem, axis=1))
    gain = g_ref[...] * (1.0 + mod_ref[SC_M:SC_M + 1, :])
    h_scr[...] = _modulated_rmsnorm(x_ref[...], gain, mod_ref[SH_M:SH_M + 1, :]).astype(BF16)

    def gelu_layernorm(acc):
        v = jax.nn.gelu(acc)
        mu = jnp.mean(v, axis=-1, keepdims=True)
        vc = v - mu
        var = jnp.mean(vc * vc, axis=-1, keepdims=True)
        return vc * lax.rsqrt(var + EPS) * lng_ref[...] + lnb_ref[...]

    identity = lambda acc: acc
    epilogues = (jax.nn.gelu, gelu_layernorm, lambda acc: acc * q_scale, identity, identity,
                 jax.nn.sigmoid, jax.nn.sigmoid)
    for j, epilogue in enumerate(epilogues):
        cols = slice(j * d, (j + 1) * d)
        for r in range(0, h_scr.shape[0], PROJ_ROW_CHUNK):
            rows = slice(r, r + PROJ_ROW_CHUNK)
            o_ref[rows, cols] = epilogue(_dot(h_scr[rows, :], w_ref[:, cols])).astype(BF16)


def _proj(x2d, mod3, norm_g, w_in, ln_g, ln_b, *, seq, tm):
    m, d = x2d.shape
    n = w_in.shape[1]
    tiles_per_batch = seq // tm
    kern = functools.partial(_proj_kernel, q_scale=MOBA_HEAD_DIM ** -0.5 * math.log2(math.e))
    const2 = lambda i: (0, 0)
    return pl.pallas_call(
        kern,
        grid=(m // tm,),
        in_specs=[
            pl.BlockSpec((tm, d), lambda i: (i, 0)),
            pl.BlockSpec((None, N_MOD, d), lambda i: (i // tiles_per_batch, 0, 0)),
            pl.BlockSpec((1, d), const2),
            pl.BlockSpec(memory_space=pl.ANY),
            pl.BlockSpec((1, d), const2),
            pl.BlockSpec((1, d), const2),
        ],
        out_specs=pl.BlockSpec((tm, n), lambda i: (i, 0)),
        out_shape=jax.ShapeDtypeStruct((m, n), BF16),
        scratch_shapes=[
            pltpu.VMEM((tm, d), BF16),
            pltpu.VMEM((d, n), BF16),
            pltpu.VMEM((WEIGHT_SLOTS, d, WEIGHT_CHUNK), F32),
            pltpu.SemaphoreType.DMA((WEIGHT_SLOTS,)),
        ],
        compiler_params=pltpu.CompilerParams(
            dimension_semantics=("arbitrary",), vmem_limit_bytes=VMEM_LIMIT),
        name="proj",
    )(x2d, mod3, norm_g, w_in, ln_g, ln_b)


ONES_ROWS = 16


def _attn_kernel(q_ref, k_ref, v_ref, o_ref, kbar_scr, vt_scr, sel_scr, s_scr, p_scr,
                 m_scr, alpha_scr, acc_scr, *, n_blocks, heads, tq):
    blk = MOBA_BLOCK
    hd = MOBA_HEAD_DIM
    qpb = tq // blk
    assert qpb == 2
    hcols = [slice(h * hd, (h + 1) * hd) for h in range(heads)]

    ones_rows = (lax.broadcasted_iota(jnp.int32, (ONES_ROWS, blk), 0) == 0).astype(F32).astype(BF16)
    for b in range(n_blocks):
        rows = slice(b * blk, (b + 1) * blk)
        kbar_scr[b:b + 1, :] = jnp.sum(k_ref[rows, :].astype(F32), axis=0, keepdims=True) * (1.0 / blk)
        for h in range(heads):
            vt_scr[b, h, :hd, :] = v_ref[rows, hcols[h]].astype(F32).T.astype(BF16)
            vt_scr[b, h, hd:, :] = ones_rows

    blk_id = lax.broadcasted_iota(jnp.int32, (n_blocks, tq), 0).astype(F32)
    col_blk = (lax.broadcasted_iota(jnp.int32, (1, tq), 1) // blk).astype(F32)
    for h in range(heads):
        kb_hi, kb_lo = _split_bf16(kbar_scr[:, hcols[h]])
        for t in range(n_blocks // qpb):
            q = q_ref[t * tq:(t + 1) * tq, hcols[h]]
            own_blk = float(t * qpb) + col_blk
            gate = _dot_nt(kb_hi, q) + _dot_nt(kb_lo, q)
            g = jnp.where(blk_id < own_blk, gate, NEG)
            selected = jnp.where(blk_id == own_blk, 1.0, 0.0)
            for _ in range(MOBA_TOPK):
                gmax = jnp.max(g, axis=0, keepdims=True)
                idx = jnp.min(jnp.where(g == gmax, blk_id, float(n_blocks)), axis=0, keepdims=True)
                pick = blk_id == idx
                selected = jnp.where(pick & (idx < own_blk), 1.0, selected)
                g = jnp.where(pick, -jnp.inf, g)
            sel_scr[h, t] = selected

    key_pos = lax.broadcasted_iota(jnp.int32, (blk, tq), 0)
    qry_pos = lax.broadcasted_iota(jnp.int32, (blk, tq), 1)

    def accumulate(kb, slot):
        for h in range(heads):
            acc_scr[h] = alpha_scr[slot, h] * acc_scr[h] + _dot(vt_scr[kb, h], p_scr[slot, h])

    def finish_tile(t):
        accumulate(jnp.where(t == 0, 1, t * qpb - 1), 1)
        rows = pl.ds(pl.multiple_of(t * tq, tq), tq)
        for h in range(heads):
            acc = acc_scr[h]
            o_ref[rows, hcols[h]] = (acc[:hd] / acc[hd:hd + 1]).T.astype(BF16)

    def q_tile(qt, carry):
        finish_tile(jnp.maximum(qt - 1, 0))
        qrows = pl.ds(pl.multiple_of(qt * tq, tq), tq)
        first_blk = qt * qpb
        for h in range(heads):
            m_scr[h] = jnp.full((1, tq), -jnp.inf, F32)
            acc_scr[h] = jnp.zeros((hd + ONES_ROWS, tq), F32)

        def scores(kb, slot, causal_off=None):
            krows = pl.ds(pl.multiple_of(kb * blk, blk), blk)
            cmax = []
            for h in range(heads):
                s = _dot_nt(k_ref[krows, hcols[h]], q_ref[qrows, hcols[h]])
                if causal_off is not None:
                    s = jnp.where(key_pos + causal_off <= qry_pos, s, NEG)
                s_scr[h] = s
                cmax.append(jnp.max(s, axis=0, keepdims=True))
            for h in range(heads):
                selected = sel_scr[h, qt, pl.ds(kb, 1), :] > 0.5
                m = m_scr[h]
                m_new = jnp.where(selected, jnp.maximum(m, cmax[h]), m)
                alpha_scr[slot, h] = jnp.where(selected, jnp.exp2(m - m_new), 1.0)
                p_scr[slot, h] = jnp.exp2(s_scr[h] - jnp.where(selected, m_new, -NEG)).astype(BF16)
                m_scr[h] = m_new

        scores(first_blk, 0, causal_off=0)
        scores(first_blk + 1, 1, causal_off=blk)
        accumulate(first_blk, 0)

        def past_pair(i, c):
            kb0 = 2 * i
            prev = jnp.where(i == 0, first_blk + 1, kb0 - 1)
            scores(kb0, 0)
            accumulate(prev, 1)
            scores(kb0 + 1, 1)
            accumulate(kb0, 0)
            return c

        lax.fori_loop(0, qt, past_pair, 0)
        return carry

    for h in range(heads):
        alpha_scr[1, h] = jnp.ones((1, tq), F32)
        p_scr[1, h] = jnp.zeros((blk, tq), BF16)
        acc_scr[h] = jnp.ones((hd + ONES_ROWS, tq), F32)
    n_tiles = n_blocks // qpb
    lax.fori_loop(0, n_tiles, q_tile, 0)
    finish_tile(n_tiles - 1)


ATTN_HEADS_PER_STEP = 4
ATTN_Q_TILE = 512


def _attention(proj, *, batch, seq, d):
    hd = MOBA_HEAD_DIM
    blk = MOBA_BLOCK
    n_blocks = seq // blk
    hg = ATTN_HEADS_PER_STEP
    tq = ATTN_Q_TILE
    groups = MOBA_HEADS // hg
    q_off, k_off, v_off = 2 * groups, 3 * groups, 4 * groups
    kern = functools.partial(_attn_kernel, n_blocks=n_blocks, heads=hg, tq=tq)
    return pl.pallas_call(
        kern,
        grid=(batch, groups),
        in_specs=[
            pl.BlockSpec((seq, hg * hd), lambda b, g: (b, q_off + g)),
            pl.BlockSpec((seq, hg * hd), lambda b, g: (b, k_off + g)),
            pl.BlockSpec((seq, hg * hd), lambda b, g: (b, v_off + g)),
        ],
        out_specs=pl.BlockSpec((seq, hg * hd), lambda b, g: (b, g)),
        out_shape=jax.ShapeDtypeStruct((batch * seq, MOBA_HEADS * hd), BF16),
        scratch_shapes=[
            pltpu.VMEM((n_blocks, hg * hd), F32),
            pltpu.VMEM((n_blocks, hg, hd + ONES_ROWS, blk), BF16),
            pltpu.VMEM((hg, seq // tq, n_blocks, tq), F32),
            pltpu.VMEM((hg, blk, tq), F32),
            pltpu.VMEM((2, hg, blk, tq), BF16),
            pltpu.VMEM((hg, 1, tq), F32),
            pltpu.VMEM((2, hg, 1, tq), F32),
            pltpu.VMEM((hg, hd + ONES_ROWS, tq), F32),
        ],
        compiler_params=pltpu.CompilerParams(
            dimension_semantics=("arbitrary", "arbitrary"), vmem_limit_bytes=VMEM_LIMIT),
        name="moba_attn",
    )(proj, proj, proj)


def _merge_kernel(x_ref, u_ref, v_ref, ga_ref, gb_ref, yb_ref, ws_ref, bs_ref,
                  wpa_hbm, wpb_hbm, wout_hbm, mod_ref, o_ref,
                  ya_scr, wpa_ref, wpb_ref, wout_ref, stage, sem, *, tm):
    @pl.when(pl.program_id(0) == 0)
    def _():
        for w_hbm, w_ref in ((wpa_hbm, wpa_ref), (wpb_hbm, wpb_ref), (wout_hbm, wout_ref)):
            _load_weight_as_bf16(w_hbm, w_ref, stage, sem, axis=1)

    t = GMLP_CHUNK
    r = lax.broadcasted_iota(jnp.int32, (t, t), 0)
    c = lax.broadcasted_iota(jnp.int32, (t, t), 1)
    tril = c <= r
    for g in range(GMLP_GROUPS):
        w = jnp.where(tril, ws_ref[g], 0.0).astype(BF16)
        bcol = bs_ref[:, g:g + 1]
        cols = slice(g * t, (g + 1) * t)
        for ch in range(tm // t):
            rows = slice(ch * t, (ch + 1) * t)
            mixed = _dot(w, v_ref[rows, cols]) + bcol
            ya_scr[rows, cols] = (u_ref[rows, cols].astype(F32) * mixed).astype(BF16)

    pa = _dot(ya_scr[...], wpa_ref[...])
    pb = _dot(yb_ref[...], wpb_ref[...])
    merged = ga_ref[...].astype(F32) * pa + gb_ref[...].astype(F32) * pb
    out = _dot(merged.astype(BF16), wout_ref[...])
    o_ref[...] = x_ref[...] + mod_ref[G_M:G_M + 1, :] * out


def _merge(x2d, proj, yb, w_sp, b_sp_t, wpa, wpb, wout, mod3, *, seq, tm):
    m, d = x2d.shape
    tiles_per_batch = seq // tm
    t = GMLP_CHUNK
    kern = functools.partial(_merge_kernel, tm=tm)
    const2 = lambda i: (0, 0)
    in_hbm = pl.BlockSpec(memory_space=pl.ANY)
    return pl.pallas_call(
        kern,
        grid=(m // tm,),
        in_specs=[
            pl.BlockSpec((tm, d), lambda i: (i, 0)),
            pl.BlockSpec((tm, d), lambda i: (i, 0)),
            pl.BlockSpec((tm, d), lambda i: (i, 1)),
            pl.BlockSpec((tm, d), lambda i: (i, 5)),
            pl.BlockSpec((tm, d), lambda i: (i, 6)),
            pl.BlockSpec((tm, d), lambda i: (i, 0)),
            pl.BlockSpec((GMLP_GROUPS, t, t), lambda i: (0, 0, 0)),
            pl.BlockSpec((t, GMLP_GROUPS), const2),
            in_hbm, in_hbm, in_hbm,
            pl.BlockSpec((None, N_MOD, d), lambda i: (i // tiles_per_batch, 0, 0)),
        ],
        out_specs=pl.BlockSpec((tm, d), lambda i: (i, 0)),
        out_shape=jax.ShapeDtypeStruct((m, d), F32),
        scratch_shapes=[
            pltpu.VMEM((tm, d), BF16),
            pltpu.VMEM((d, d), BF16), pltpu.VMEM((d, d), BF16), pltpu.VMEM((d, d), BF16),
            pltpu.VMEM((WEIGHT_SLOTS // 2, d, WEIGHT_CHUNK), F32),
            pltpu.SemaphoreType.DMA((WEIGHT_SLOTS // 2,)),
        ],
        compiler_params=pltpu.CompilerParams(
            dimension_semantics=("arbitrary",), vmem_limit_bytes=VMEM_LIMIT),
        name="merge",
    )(x2d, proj, proj, proj, proj, yb, w_sp, b_sp_t, wpa, wpb, wout, mod3)


FFN_ROW_CHUNK = 256


def _ffn_kernel(x_ref, mod_ref, g_ref, wg_hbm, wu_hbm, wd_hbm, gfin_ref, o_ref,
                wg_ref, wu_ref, wd_ref, stage_cols, stage_rows, sem, *, final_norm):
    @pl.when(pl.program_id(0) == 0)
    def _():
        _load_weight_as_bf16(wg_hbm, wg_ref, stage_cols, sem, axis=1)
        _load_weight_as_bf16(wu_hbm, wu_ref, stage_cols, sem, axis=1)
        _load_weight_as_bf16(wd_hbm, wd_ref, stage_rows, sem, axis=0)

    gain = g_ref[...] * (1.0 + mod_ref[SC_F:SC_F + 1, :])
    for r in range(0, x_ref.shape[0], FFN_ROW_CHUNK):
        rows = slice(r, r + FFN_ROW_CHUNK)
        x = x_ref[rows, :]
        h = _modulated_rmsnorm(x, gain, mod_ref[SH_F:SH_F + 1, :]).astype(BF16)
        gate = _dot(h, wg_ref[...])
        up = _dot(h, wu_ref[...])
        a = (gate * jax.nn.sigmoid(gate) * up).astype(BF16)
        y = x + mod_ref[G_F:G_F + 1, :] * _dot(a, wd_ref[...])
        if final_norm:
            y = y * lax.rsqrt(jnp.mean(y * y, axis=-1, keepdims=True) + EPS) * gfin_ref[...]
        o_ref[rows, :] = y


def _ffn(x2d, mod3, norm_g, wg, wu, wd, g_final, *, seq, tm, final_norm):
    m, d = x2d.shape
    fh = wg.shape[1]
    tiles_per_batch = seq // tm
    kern = functools.partial(_ffn_kernel, final_norm=final_norm)
    const2 = lambda i: (0, 0)
    in_hbm = pl.BlockSpec(memory_space=pl.ANY)
    return pl.pallas_call(
        kern,
        grid=(m // tm,),
        in_specs=[
            pl.BlockSpec((tm, d), lambda i: (i, 0)),
            pl.BlockSpec((None, N_MOD, d), lambda i: (i // tiles_per_batch, 0, 0)),
            pl.BlockSpec((1, d), const2),
            in_hbm, in_hbm, in_hbm,
            pl.BlockSpec((1, d), const2),
        ],
        out_specs=pl.BlockSpec((tm, d), lambda i: (i, 0)),
        out_shape=jax.ShapeDtypeStruct((m, d), F32),
        scratch_shapes=[
            pltpu.VMEM((d, fh), BF16), pltpu.VMEM((d, fh), BF16), pltpu.VMEM((fh, d), BF16),
            pltpu.VMEM((WEIGHT_SLOTS, d, WEIGHT_CHUNK), F32),
            pltpu.VMEM((WEIGHT_SLOTS, WEIGHT_CHUNK, d), F32),
            pltpu.SemaphoreType.DMA((WEIGHT_SLOTS,)),
        ],
        compiler_params=pltpu.CompilerParams(
            dimension_semantics=("arbitrary",), vmem_limit_bytes=VMEM_LIMIT),
        name="ffn",
    )(x2d, mod3, norm_g, wg, wu, wd, g_final)


def kernel(x, c, w_ada, b_ada, norm_mix_g, w_in, ln_v_g, ln_v_b, w_spatial, b_spatial,
           w_proj_a, w_proj_b, w_out, norm_ffn_g, w_ffn_gate, w_ffn_up, w_ffn_down,
           norm_final_g):
    batch, seq, d = x.shape
    depth = w_ada.shape[0]
    assert d == MOBA_HEADS * MOBA_HEAD_DIM == GMLP_GROUPS * LANES
    assert seq % ATTN_Q_TILE == 0 and ATTN_Q_TILE % MOBA_BLOCK == 0
    assert w_in.shape[2] == N_SECTIONS * d

    x2d = x.reshape(batch * seq, d)
    row = lambda a: a.reshape(1, -1)
    g_final = row(norm_final_g)

    for l in range(depth):
        mod3 = _adaln(c, w_ada[l], row(b_ada[l])).reshape(batch, N_MOD, d)
        proj = _proj(x2d, mod3, row(norm_mix_g[l]), w_in[l],
                     row(ln_v_g[l]), row(ln_v_b[l]), seq=seq, tm=512)
        yb = _attention(proj, batch=batch, seq=seq, d=d)
        x2d = _merge(x2d, proj, yb, w_spatial[l], b_spatial[l].T,
                     w_proj_a[l], w_proj_b[l], w_out[l],
                     mod3, seq=seq, tm=1024)
        x2d = _ffn(x2d, mod3, row(norm_ffn_g[l]), w_ffn_gate[l], w_ffn_up[l], w_ffn_down[l], g_final,
                   seq=seq, tm=1024, final_norm=(l == depth - 1))
    return x2d.reshape(batch, seq, d)
```
